```python
import jax, jax.numpy as jnp
from jax import lax
import numpy as np

D_MODEL = 1024
BATCH = 2
SEQ = 8192
DEPTH = 4

CHUNK = 64
N_META = 16
N_PAD = CHUNK - N_META
N_MIXERS = 2
RMS_EPS = 1e-6
L2_EPS = 1e-6

GDN_QK_HEADS = 8
GDN_V_HEADS = 16
GDN_HEAD_DIM = 128
GDN_CONV = 4
GDN_KEY_DIM = GDN_QK_HEADS * GDN_HEAD_DIM
GDN_VAL_DIM = GDN_V_HEADS * GDN_HEAD_DIM
GDN_QKV_DIM = 2 * GDN_KEY_DIM + GDN_VAL_DIM
GDN_IN = GDN_QKV_DIM + GDN_VAL_DIM + 2 * GDN_V_HEADS

ML_HEADS = 4
ML_QK_DIM = 128
ML_V_DIM = 256
ML_GATE_CAP = 15.0
ML_QK_TOT = ML_HEADS * ML_QK_DIM
ML_V_TOT = ML_HEADS * ML_V_DIM
ML_IN = 2 * ML_QK_TOT + 2 * ML_V_TOT + 2 * ML_HEADS

FFN_DIM = 2816
FFN_CONV = 3

N_GDN_LAYERS = (DEPTH + 1) // 2
N_ML_LAYERS = DEPTH // 2

kernel_name = "hybrid_gdn_mlstm_convffn_trunk"


def rms_norm(x, w):
    xf = x.astype(jnp.float32)
    y = xf * lax.rsqrt(jnp.mean(xf * xf, -1, keepdims=True) + RMS_EPS)
    return (y * w.astype(jnp.float32)).astype(x.dtype)


def l2_normalize(x):
    return x * lax.rsqrt(jnp.sum(x * x, -1, keepdims=True) + L2_EPS)


def causal_depthwise_conv(x, w):
    K, C = w.shape
    return lax.conv_general_dilated(x, w[:, None, :].astype(x.dtype), window_strides=(1,),
                                    padding=[(K - 1, 0)], dimension_numbers=('NWC', 'WIO', 'NWC'),
                                    feature_group_count=C)


def to_chunks(t):
    B, L, H = t.shape[:3]
    t = t.reshape((B, L // CHUNK, CHUNK, H) + t.shape[3:])
    return jnp.moveaxis(t, 3, 1)


def from_chunks(t):
    B, H, nc, C, d = t.shape
    return jnp.moveaxis(t, 1, 3).reshape(B, nc * C, H, d)


def gated_deltanet(h, valid, w_in, conv_w, a_log, dt_bias, norm_w, w_out):
    f32 = jnp.float32
    B, L, _ = h.shape
    proj = h @ w_in
    o1 = GDN_QKV_DIM
    o2 = o1 + GDN_VAL_DIM
    o3 = o2 + GDN_V_HEADS
    qkv = jax.nn.silu(causal_depthwise_conv(proj[..., :o1], conv_w)).astype(f32)
    z = proj[..., o1:o2].astype(f32).reshape(B, L, GDN_V_HEADS, GDN_HEAD_DIM)
    beta = jax.nn.sigmoid(proj[..., o2:o3].astype(f32))
    g = -jnp.exp(a_log.astype(f32)) * jax.nn.softplus(proj[..., o3:].astype(f32) + dt_bias.astype(f32))

    rep = GDN_V_HEADS // GDN_QK_HEADS
    mask = valid[None, :, None, None]
    q = qkv[..., :GDN_KEY_DIM].reshape(B, L, GDN_QK_HEADS, GDN_HEAD_DIM)
    k = qkv[..., GDN_KEY_DIM:2 * GDN_KEY_DIM].reshape(B, L, GDN_QK_HEADS, GDN_HEAD_DIM)
    v = qkv[..., 2 * GDN_KEY_DIM:].reshape(B, L, GDN_V_HEADS, GDN_HEAD_DIM) * mask
    q = jnp.repeat(l2_normalize(q), rep, axis=2) * (GDN_HEAD_DIM ** -0.5)
    k = jnp.repeat(l2_normalize(k), rep, axis=2) * mask

    qc, kc, vc = to_chunks(q), to_chunks(k), to_chunks(v)
    bc, gc = to_chunks(beta), to_chunks(g)
    G = jnp.cumsum(gc, -1)
    incl = jnp.tri(CHUNK, dtype=bool)
    strict = jnp.tri(CHUNK, k=-1, dtype=f32)
    decay = jnp.exp(jnp.where(incl, G[..., :, None] - G[..., None, :], -jnp.inf))
    kb = kc * bc[..., None]
    a_mat = jnp.eye(CHUNK, dtype=f32) + jnp.einsum('bhnid,bhnjd->bhnij', kb, kc) * decay * strict
    rhs = jnp.concatenate([vc * bc[..., None], kb * jnp.exp(G)[..., None]], -1)
    sol = lax.linalg.triangular_solve(a_mat, rhs, left_side=True, lower=True)
    u, w = sol[..., :GDN_HEAD_DIM], sol[..., GDN_HEAD_DIM:]
    attn = jnp.einsum('bhnid,bhnjd->bhnij', qc, kc) * decay
    qd = qc * jnp.exp(G)[..., None]
    kt = kc * jnp.exp(G[..., -1:] - G)[..., None]
    gt = jnp.exp(G[..., -1])

    def step(S, xs):
        u_c, w_c, qd_c, att_c, kt_c, gt_c = xs
        v_new = u_c - jnp.einsum('bhck,bhkv->bhcv', w_c, S)
        o_c = jnp.einsum('bhck,bhkv->bhcv', qd_c, S) + jnp.einsum('bhij,bhjv->bhiv', att_c, v_new)
        S = S * gt_c[..., None, None] + jnp.einsum('bhck,bhcv->bhkv', kt_c, v_new)
        return S, o_c

    xs = tuple(jnp.moveaxis(t, 2, 0) for t in (u, w, qd, attn, kt, gt))
    S0 = jnp.zeros((B, GDN_V_HEADS, GDN_HEAD_DIM, GDN_HEAD_DIM), f32)
    _, o = lax.scan(step, S0, xs)
    o = from_chunks(jnp.moveaxis(o, 0, 2))
    o = rms_norm(o, norm_w) * jax.nn.silu(z)
    return o.reshape(B, L, GDN_VAL_DIM).astype(h.dtype) @ w_out


def mlstm(h, valid, w_in, gate_b, norm_w, w_out):
    f32 = jnp.float32
    B, L, _ = h.shape
    proj = h @ w_in
    o1 = ML_QK_TOT
    o2 = 2 * ML_QK_TOT
    o3 = o2 + ML_V_TOT
    o4 = o3 + ML_V_TOT
    mask = valid[None, :, None, None]
    q = proj[..., :o1].astype(f32).reshape(B, L, ML_HEADS, ML_QK_DIM) * (ML_QK_DIM ** -0.5)
    k = proj[..., o1:o2].astype(f32).reshape(B, L, ML_HEADS, ML_QK_DIM) * mask
    v = proj[..., o2:o3].astype(f32).reshape(B, L, ML_HEADS, ML_V_DIM) * mask
    o_gate = jax.nn.sigmoid(proj[..., o3:o4].astype(f32)).reshape(B, L, ML_HEADS, ML_V_DIM)
    gates = proj[..., o4:].astype(f32) + gate_b.astype(f32)
    gates = ML_GATE_CAP * jnp.tanh(gates / ML_GATE_CAP)
    log_i = gates[..., :ML_HEADS]
    log_f = jax.nn.log_sigmoid(gates[..., ML_HEADS:])

    qc, kc, vc = to_chunks(q), to_chunks(k), to_chunks(v)
    lic, lfc = to_chunks(log_i), to_chunks(log_f)
    b = jnp.cumsum(lfc, -1)
    b_last = b[..., -1]
    a = b_last[..., None] - b + lic

    def step(carry, xs):
        Cs, ns, ms = carry
        k_c, v_c, a_c, bl_c = xs
        m_new = jnp.maximum(bl_c + ms, jnp.max(a_c, -1))
        dec = jnp.exp(bl_c + ms - m_new)
        kw = k_c * jnp.exp(a_c - m_new[..., None])[..., None]
        C_new = dec[..., None, None] * Cs + jnp.einsum('bhck,bhcv->bhkv', kw, v_c)
        n_new = dec[..., None] * ns + jnp.sum(kw, -2)
        return (C_new, n_new, m_new), (Cs, ns, ms)

    xs = tuple(jnp.moveaxis(t, 2, 0) for t in (kc, vc, a, b_last))
    init = (jnp.zeros((B, ML_HEADS, ML_QK_DIM, ML_V_DIM), f32),
            jnp.zeros((B, ML_HEADS, ML_QK_DIM), f32),
            jnp.zeros((B, ML_HEADS), f32))
    _, (C_st, n_st, m_st) = lax.scan(step, init, xs)
    C_st = jnp.moveaxis(C_st, 0, 2)
    n_st = jnp.moveaxis(n_st, 0, 2)
    m_st = jnp.moveaxis(m_st, 0, 2)

    incl = jnp.tri(CHUNK, dtype=bool)
    D = jnp.where(incl, b[..., :, None] - b[..., None, :] + lic[..., None, :], -jnp.inf)
    inter = b + m_st[..., None]
    m_t = jnp.maximum(jnp.max(D, -1), inter)
    wD = jnp.exp(D - m_t[..., None]) * jnp.einsum('bhnid,bhnjd->bhnij', qc, kc)
    sc = jnp.exp(inter - m_t)
    num = sc[..., None] * jnp.einsum('bhnck,bhnkv->bhncv', qc, C_st) + jnp.einsum('bhnij,bhnjv->bhniv', wD, vc)
    den = sc * jnp.einsum('bhnck,bhnk->bhnc', qc, n_st) + jnp.sum(wD, -1)
    hh = num / jnp.maximum(jnp.abs(den), jnp.exp(-m_t))[..., None]
    hh = from_chunks(hh)
    hh = rms_norm(hh, norm_w.reshape(ML_HEADS, ML_V_DIM)) * o_gate
    return hh.reshape(B, L, ML_V_TOT).astype(h.dtype) @ w_out


def conv_ffn(h, w_up, conv_w, conv_b, w_down):
    u = causal_depthwise_conv(h @ w_up, conv_w) + conv_b
    gate, up = u[..., :FFN_DIM], u[..., FFN_DIM:]
    return (jax.nn.silu(gate) * up) @ w_down


def setup_inputs(seed: int = 0) -> dict:
    key = jax.random.key(seed)
    ks = jax.random.split(key, 20)
    f32 = jnp.float32
    nrm = lambda k, shape, s: jax.random.normal(k, shape, f32) * s
    x = nrm(ks[0], (BATCH, SEQ, D_MODEL), 1.0)
    meta_tokens = nrm(ks[1], (N_META, D_MODEL), 1.0)
    norm_w = 1.0 + nrm(ks[2], (DEPTH, 4, D_MODEL), 0.05)
    gdn_w_in = nrm(ks[3], (N_GDN_LAYERS, D_MODEL, GDN_IN), D_MODEL ** -0.5)
    gdn_conv_w = nrm(ks[4], (N_GDN_LAYERS, GDN_CONV, GDN_QKV_DIM), GDN_CONV ** -0.5)
    gdn_a_log = jnp.log(jax.random.uniform(ks[5], (N_GDN_LAYERS, GDN_V_HEADS), f32, 1.0, 16.0))
    dt = jnp.exp(jax.random.uniform(ks[6], (N_GDN_LAYERS, GDN_V_HEADS), f32, np.log(1e-3), np.log(1e-1)))
    gdn_dt_bias = dt + jnp.log(-jnp.expm1(-dt))
    gdn_norm_w = 1.0 + nrm(ks[7], (N_GDN_LAYERS, GDN_HEAD_DIM), 0.05)
    gdn_w_out = nrm(ks[8], (N_GDN_LAYERS, GDN_VAL_DIM, D_MODEL), GDN_VAL_DIM ** -0.5)
    ml_w_in = nrm(ks[9], (N_ML_LAYERS, D_MODEL, ML_IN), D_MODEL ** -0.5)
    ig_b = nrm(ks[10], (N_ML_LAYERS, ML_HEADS), 0.1)
    fg_b = jnp.linspace(3.0, 6.0, ML_HEADS, dtype=f32)[None] + nrm(ks[11], (N_ML_LAYERS, ML_HEADS), 0.1)
    ml_gate_b = jnp.concatenate([ig_b, fg_b], -1)
    ml_norm_w = 1.0 + nrm(ks[12], (N_ML_LAYERS, ML_V_TOT), 0.05)
    ml_w_out = nrm(ks[13], (N_ML_LAYERS, ML_V_TOT, D_MODEL), ML_V_TOT ** -0.5)
    ffn_w_up = nrm(ks[14], (DEPTH, D_MODEL, 2 * FFN_DIM), D_MODEL ** -0.5)
    ffn_conv_w = nrm(ks[15], (DEPTH, FFN_CONV, 2 * FFN_DIM), FFN_CONV ** -0.5)
    ffn_conv_b = nrm(ks[16], (DEPTH, 2 * FFN_DIM), 0.01)
    ffn_w_down = nrm(ks[17], (DEPTH, FFN_DIM, D_MODEL), FFN_DIM ** -0.5)
    return {"x": x, "meta_tokens": meta_tokens, "norm_w": norm_w,
            "gdn_w_in": gdn_w_in, "gdn_conv_w": gdn_conv_w, "gdn_a_log": gdn_a_log,
            "gdn_dt_bias": gdn_dt_bias, "gdn_norm_w": gdn_norm_w, "gdn_w_out": gdn_w_out,
            "ml_w_in": ml_w_in, "ml_gate_b": ml_gate_b, "ml_norm_w": ml_norm_w, "ml_w_out": ml_w_out,
            "ffn_w_up": ffn_w_up, "ffn_conv_w": ffn_conv_w, "ffn_conv_b": ffn_conv_b, "ffn_w_down": ffn_w_down}


def reference(x, meta_tokens, norm_w, gdn_w_in, gdn_conv_w, gdn_a_log, gdn_dt_bias, gdn_norm_w, gdn_w_out,
              ml_w_in, ml_gate_b, ml_norm_w, ml_w_out, ffn_w_up, ffn_conv_w, ffn_conv_b, ffn_w_down):
    B, S, D = x.shape
    pad = jnp.zeros((B, N_PAD, D), x.dtype)
    meta = jnp.broadcast_to(meta_tokens[None].astype(x.dtype), (B, N_META, D))
    hs = jnp.concatenate([pad, meta, x], 1)
    L = hs.shape[1]
    valid = (jnp.arange(L) >= N_PAD).astype(jnp.float32)
    keep = valid.astype(x.dtype)[None, :, None]
    for i in range(DEPTH):
        j = i // N_MIXERS
        a_in = rms_norm(hs, norm_w[i, 0])
        if i % N_MIXERS == 0:
            mix = gated_deltanet(a_in, valid, gdn_w_in[j], gdn_conv_w[j], gdn_a_log[j], gdn_dt_bias[j],
                                 gdn_norm_w[j], gdn_w_out[j])
        else:
            mix = mlstm(a_in, valid, ml_w_in[j], ml_gate_b[j], ml_norm_w[j], ml_w_out[j])
        hs = hs + keep * rms_norm(mix, norm_w[i, 1])
        f = conv_ffn(rms_norm(hs, norm_w[i, 2]), ffn_w_up[i], ffn_conv_w[i], ffn_conv_b[i], ffn_w_down[i])
        hs = hs + keep * rms_norm(f, norm_w[i, 3])
    return hs[:, N_PAD + N_META:]
```

```python
import functools

import jax
import jax.numpy as jnp
from jax import lax
from jax.experimental import pallas as pl
from jax.experimental.pallas import tpu as pltpu

F32 = jnp.float32
BF16 = jnp.bfloat16

CHUNK = 64
N_META = 16
N_PAD = CHUNK - N_META
RMS_EPS = 1e-6
L2_EPS = 1e-6

GDN_QK_HEADS = 8
GDN_V_HEADS = 16
GDN_HEAD_DIM = 128
GDN_CONV = 4
GDN_KEY_DIM = GDN_QK_HEADS * GDN_HEAD_DIM
GDN_VAL_DIM = GDN_V_HEADS * GDN_HEAD_DIM
GDN_QKV_DIM = 2 * GDN_KEY_DIM + GDN_VAL_DIM
GDN_MAIN = GDN_QKV_DIM + GDN_VAL_DIM

ML_HEADS = 4
ML_QK_DIM = 128
ML_V_DIM = 256
ML_GATE_CAP = 15.0
ML_QK_TOT = ML_HEADS * ML_QK_DIM
ML_V_TOT = ML_HEADS * ML_V_DIM
ML_MAIN = 2 * ML_QK_TOT + 2 * ML_V_TOT

FFN_CONV = 3
FFN_BLOCK = 256

LANES = 128
SUBLANES = 8
SOLVE_UNITS = 128
VMEM_LIMIT = 56 * 1024 * 1024


def _largest_tile(n, cap):
    d = (min(cap, n) // SUBLANES) * SUBLANES
    while d > SUBLANES and n % d:
        d -= SUBLANES
    assert d >= SUBLANES and n % d == 0, (n, cap)
    return d


def _params(*sem):
    return pltpu.CompilerParams(dimension_semantics=sem, vmem_limit_bytes=VMEM_LIMIT)


def _rms(x, w):
    return x * lax.rsqrt(jnp.mean(x * x, -1, keepdims=True) + RMS_EPS) * w


def _silu(x):
    return x * jax.nn.sigmoid(x)


def _dot(a, b):
    return jnp.dot(a, b, preferred_element_type=F32)


def _dot_nt(a, b):
    return lax.dot_general(a, b, (((1,), (1,)), ((), ())), preferred_element_type=F32)


def _dot_tn(a, b):
    return lax.dot_general(a, b, (((0,), (0,)), ((), ())), preferred_element_type=F32)


def _dot_f32(a, b):
    return jnp.dot(a, b, preferred_element_type=F32, precision=lax.Precision.HIGHEST)


def _tri_incl():
    r = lax.broadcasted_iota(jnp.int32, (CHUNK, CHUNK), 0)
    c = lax.broadcasted_iota(jnp.int32, (CHUNK, CHUNK), 1)
    return r, c


def _inproj_kernel(x_ref, nw_ref, wm_ref, wg_ref, om_ref, og_ref, a_scr):
    @pl.when(pl.program_id(1) == 0)
    def _():
        a = _rms(x_ref[...], nw_ref[...]).astype(BF16)
        a_scr[...] = a
        og_ref[...] = _dot(a, wg_ref[...])

    om_ref[...] = _dot(a_scr[...], wm_ref[...])


def _inproj(x2d, nw, w_main, w_gate, tn):
    m, d = x2d.shape
    n = w_main.shape[1]
    tm = _largest_tile(m, 1400)
    return pl.pallas_call(
        _inproj_kernel,
        grid=(m // tm, n // tn),
        in_specs=[
            pl.BlockSpec((tm, d), lambda i, j: (i, 0)),
            pl.BlockSpec((1, d), lambda i, j: (0, 0)),
            pl.BlockSpec((d, tn), lambda i, j: (0, j)),
            pl.BlockSpec((d, LANES), lambda i, j: (0, 0)),
        ],
        out_specs=[
            pl.BlockSpec((tm, tn), lambda i, j: (i, j)),
            pl.BlockSpec((tm, LANES), lambda i, j: (i, 0)),
        ],
        out_shape=[jax.ShapeDtypeStruct((m, n), F32), jax.ShapeDtypeStruct((m, LANES), F32)],
        scratch_shapes=[pltpu.VMEM((tm, d), BF16)],
        compiler_params=_params("parallel", "arbitrary"),
        name="inproj",
    )(x2d, nw, w_main, w_gate)


def _outproj_kernel(o_ref, w_ref, nw_ref, hs_ref, out_ref, *, tm, seq_len):
    y = _rms(_dot(o_ref[...], w_ref[...]), nw_ref[...])
    t = (pl.program_id(0) * tm) % seq_len + lax.broadcasted_iota(jnp.int32, (tm, 1), 0)
    out_ref[...] = hs_ref[...] + jnp.where(t >= N_PAD, y, 0.0)


def _outproj(o2d, w_out, nw, hs2d, seq_len):
    m, kin = o2d.shape
    d = hs2d.shape[1]
    tm = _largest_tile(seq_len, 1400)
    return pl.pallas_call(
        functools.partial(_outproj_kernel, tm=tm, seq_len=seq_len),
        grid=(m // tm,),
        in_specs=[
            pl.BlockSpec((tm, kin), lambda i: (i, 0)),
            pl.BlockSpec((kin, d), lambda i: (0, 0)),
            pl.BlockSpec((1, d), lambda i: (0, 0)),
            pl.BlockSpec((tm, d), lambda i: (i, 0)),
        ],
        out_specs=pl.BlockSpec((tm, d), lambda i: (i, 0)),
        out_shape=jax.ShapeDtypeStruct((m, d), F32),
        compiler_params=_params("parallel"),
        name="outproj",
    )(o2d, w_out, nw, hs2d)


def _ffn_kernel(h_ref, halo_ref, nw2_ref, wup_ref, cw_ref, cb_ref, wdn_ref, nw3_ref, out_ref, *, tm, nblk):
    i = pl.program_id(1)
    h = h_ref[0]
    halo = jnp.where(i == 0, 0.0, halo_ref[0])
    a = _rms(jnp.concatenate([halo, h], 0), nw2_ref[...]).astype(BF16)

    def body(j, acc):
        u = _dot(a, wup_ref[j])
        cw = cw_ref[j]
        y = (cw[2:3] * u[SUBLANES:]
             + cw[1:2] * pltpu.roll(u, 1, 0)[SUBLANES:]
             + cw[0:1] * pltpu.roll(u, 2, 0)[SUBLANES:]
             + cb_ref[j])
        act = (_silu(y[:, :FFN_BLOCK]) * y[:, FFN_BLOCK:]).astype(BF16)
        return acc + _dot(act, wdn_ref[j])

    f = lax.fori_loop(0, nblk, body, jnp.zeros((tm, h.shape[1]), F32))
    t = i * tm + lax.broadcasted_iota(jnp.int32, (tm, 1), 0)
    out_ref[0] = h + jnp.where(t >= N_PAD, _rms(f, nw3_ref[...]), 0.0)


def _ffn(hs, nw2, wup_r, cw_r, cb_r, wdn_r, nw3):
    b, seq_len, d = hs.shape
    nblk = wup_r.shape[0]
    tm = _largest_tile(seq_len, 700)
    hb = tm // SUBLANES
    return pl.pallas_call(
        functools.partial(_ffn_kernel, tm=tm, nblk=nblk),
        grid=(b, seq_len // tm),
        in_specs=[
            pl.BlockSpec((1, tm, d), lambda bi, i: (bi, i, 0)),
            pl.BlockSpec((1, SUBLANES, d), lambda bi, i: (bi, jnp.maximum(i * hb - 1, 0), 0)),
            pl.BlockSpec((1, d), lambda bi, i: (0, 0)),
            pl.BlockSpec(wup_r.shape, lambda bi, i: (0, 0, 0)),
            pl.BlockSpec(cw_r.shape, lambda bi, i: (0, 0, 0)),
            pl.BlockSpec(cb_r.shape, lambda bi, i: (0, 0, 0)),
            pl.BlockSpec(wdn_r.shape, lambda bi, i: (0, 0, 0)),
            pl.BlockSpec((1, d), lambda bi, i: (0, 0)),
        ],
        out_specs=pl.BlockSpec((1, tm, d), lambda bi, i: (bi, i, 0)),
        out_shape=jax.ShapeDtypeStruct(hs.shape, F32),
        compiler_params=_params("parallel", "parallel"),
        name="convffn",
    )(hs, hs, nw2, wup_r, cw_r, cb_r, wdn_r, nw3)


def _gdn_gates_col(graw, prm_ref):
    beta = jax.nn.sigmoid(graw)
    g = -jnp.exp(prm_ref[0:1, :]) * jax.nn.softplus(graw + prm_ref[1:2, :])
    return beta, g


def _gdn_prep_kernel(qk_ref, v_ref, qkh_ref, vh_ref, cwqk_ref, cwv_ref, gc_ref, gr_ref, prow_ref, pcol_ref,
                     qn_ref, kn_ref, vv_ref, gco_ref, gro_ref, a_ref, *, nchunk):
    i = pl.program_id(1)
    rows = nchunk * CHUNK
    t = i * rows + lax.broadcasted_iota(jnp.int32, (rows, 1), 0)
    valid = (t >= N_PAD).astype(F32)
    first = i == 0

    def conv_silu(x_ref, halo_ref, cw_ref, col):
        sl = slice(col * GDN_HEAD_DIM, (col + 1) * GDN_HEAD_DIM)
        halo = jnp.where(first, 0.0, halo_ref[0, :, sl])
        x = jnp.concatenate([halo, x_ref[0, :, sl]], 0)
        cw = cw_ref[:, sl]
        y = (cw[3:4] * x[SUBLANES:]
             + cw[2:3] * pltpu.roll(x, 1, 0)[SUBLANES:]
             + cw[1:2] * pltpu.roll(x, 2, 0)[SUBLANES:]
             + cw[0:1] * pltpu.roll(x, 3, 0)[SUBLANES:])
        return _silu(y)

    def l2n(x):
        return x * lax.rsqrt(jnp.sum(x * x, -1, keepdims=True) + L2_EPS)

    for hq in range(GDN_QK_HEADS):
        sl = slice(hq * GDN_HEAD_DIM, (hq + 1) * GDN_HEAD_DIM)
        qn_ref[0, :, sl] = l2n(conv_silu(qk_ref, qkh_ref, cwqk_ref, hq)) * (GDN_HEAD_DIM ** -0.5)
        kn_ref[0, :, sl] = l2n(conv_silu(qk_ref, qkh_ref, cwqk_ref, GDN_QK_HEADS + hq)) * valid
    for h in range(GDN_V_HEADS):
        sl = slice(h * GDN_HEAD_DIM, (h + 1) * GDN_HEAD_DIM)
        vv_ref[0, :, sl] = conv_silu(v_ref, vh_ref, cwv_ref, h) * valid

    r, c = _tri_incl()
    tri = (r >= c).astype(F32)
    tri_t = (r <= c).astype(F32)
    incl = r >= c
    strict = r > c
    rep = GDN_V_HEADS // GDN_QK_HEADS
    for ci in range(nchunk):
        rs = slice(ci * CHUNK, (ci + 1) * CHUNK)
        beta, g = _gdn_gates_col(gc_ref[0, rs, :], prow_ref)
        gcum = _dot_f32(tri, g)
        lane = lax.broadcasted_iota(jnp.int32, (CHUNK, LANES), 1)
        gco_ref[0, rs, :] = jnp.where(lane < GDN_V_HEADS, beta, gcum)
        graw_r = gr_ref[0, ci]
        g_r = -jnp.exp(pcol_ref[:, 0:1]) * jax.nn.softplus(graw_r + pcol_ref[:, 1:2])
        gcum_r = _dot_f32(g_r, tri_t)
        gro_ref[0, ci] = gcum_r
        for hq in range(GDN_QK_HEADS):
            kc = kn_ref[0, rs, hq * GDN_HEAD_DIM:(hq + 1) * GDN_HEAD_DIM].astype(BF16)
            kk = _dot_nt(kc, kc)
            for e in range(rep):
                h = hq * rep + e
                gi = gcum[:, GDN_V_HEADS + h:GDN_V_HEADS + h + 1]
                gj = gcum_r[GDN_V_HEADS + h:GDN_V_HEADS + h + 1, :]
                dec = jnp.exp(jnp.where(incl, gi - gj, -jnp.inf))
                a_ref[0, ci, h] = jnp.where(strict, kk * beta[:, h:h + 1] * dec, 0.0)


def _gdn_prep(proj, gates, gates_r, conv_w, prm_row, prm_col, nchunk):
    b, seq_len, _ = proj.shape
    nc = seq_len // CHUNK
    rows = nchunk * CHUNK
    hb = rows // SUBLANES
    half = GDN_QKV_DIM // 2
    halo_map0 = lambda bi, i: (bi, jnp.maximum(i * hb - 1, 0), 0)
    halo_map1 = lambda bi, i: (bi, jnp.maximum(i * hb - 1, 0), 1)
    return pl.pallas_call(
        functools.partial(_gdn_prep_kernel, nchunk=nchunk),
        grid=(b, nc // nchunk),
        in_specs=[
            pl.BlockSpec((1, rows, half), lambda bi, i: (bi, i, 0)),
            pl.BlockSpec((1, rows, half), lambda bi, i: (bi, i, 1)),
            pl.BlockSpec((1, SUBLANES, half), halo_map0),
            pl.BlockSpec((1, SUBLANES, half), halo_map1),
            pl.BlockSpec((GDN_CONV, half), lambda bi, i: (0, 0)),
            pl.BlockSpec((GDN_CONV, half), lambda bi, i: (0, 1)),
            pl.BlockSpec((1, rows, LANES), lambda bi, i: (bi, i, 0)),
            pl.BlockSpec((1, nchunk, 2 * GDN_V_HEADS, CHUNK), lambda bi, i: (bi, i, 0, 0)),
            pl.BlockSpec((SUBLANES, LANES), lambda bi, i: (0, 0)),
            pl.BlockSpec((2 * GDN_V_HEADS, LANES), lambda bi, i: (0, 0)),
        ],
        out_specs=[
            pl.BlockSpec((1, rows, GDN_KEY_DIM), lambda bi, i: (bi, i, 0)),
            pl.BlockSpec((1, rows, GDN_KEY_DIM), lambda bi, i: (bi, i, 0)),
            pl.BlockSpec((1, rows, GDN_VAL_DIM), lambda bi, i: (bi, i, 0)),
            pl.BlockSpec((1, rows, LANES), lambda bi, i: (bi, i, 0)),
            pl.BlockSpec((1, nchunk, 2 * GDN_V_HEADS, CHUNK), lambda bi, i: (bi, i, 0, 0)),
            pl.BlockSpec((1, nchunk, GDN_V_HEADS, CHUNK, CHUNK), lambda bi, i: (bi, i, 0, 0, 0)),
        ],
        out_shape=[
            jax.ShapeDtypeStruct((b, seq_len, GDN_KEY_DIM), F32),
            jax.ShapeDtypeStruct((b, seq_len, GDN_KEY_DIM), F32),
            jax.ShapeDtypeStruct((b, seq_len, GDN_VAL_DIM), F32),
            jax.ShapeDtypeStruct((b, seq_len, LANES), F32),
            jax.ShapeDtypeStruct((b, nc, 2 * GDN_V_HEADS, CHUNK), F32),
            jax.ShapeDtypeStruct((b, nc, GDN_V_HEADS, CHUNK, CHUNK), F32),
        ],
        compiler_params=_params("parallel", "parallel"),
        name="gdn_prep",
    )(proj, proj, proj, proj, conv_w, conv_w, gates, gates_r, prm_row, prm_col)


def _tri_inv_kernel(a_ref, x_ref, at_scr, xt_scr):
    per = LANES // CHUNK
    for cb in range(CHUNK // per):
        tile = a_ref[:, cb * LANES:(cb + 1) * LANES].T
        for p in range(per):
            at_scr[cb * per + p] = tile[p * CHUNK:(p + 1) * CHUNK]

    jrow = lax.broadcasted_iota(jnp.int32, (CHUNK, SOLVE_UNITS), 0)
    xt_scr[0] = (jrow == 0).astype(F32)

    def row_body(i, carry):
        def k_body(k, acc):
            return acc - at_scr[i, pl.ds(k, 1), :] * xt_scr[k]
        xt_scr[i] = lax.fori_loop(0, i, k_body, (jrow == i).astype(F32))
        return carry

    lax.fori_loop(1, CHUNK, row_body, 0)

    for cb in range(CHUNK // per):
        tile = jnp.concatenate([xt_scr[cb * per + p] for p in range(per)], 0)
        x_ref[:, cb * LANES:(cb + 1) * LANES] = tile.T


def _tri_inv(a_flat):
    u, e = a_flat.shape
    return pl.pallas_call(
        _tri_inv_kernel,
        grid=(u // SOLVE_UNITS,),
        in_specs=[pl.BlockSpec((SOLVE_UNITS, e), lambda i: (i, 0))],
        out_specs=pl.BlockSpec((SOLVE_UNITS, e), lambda i: (i, 0)),
        out_shape=jax.ShapeDtypeStruct((u, e), F32),
        scratch_shapes=[pltpu.VMEM((CHUNK, CHUNK, SOLVE_UNITS), F32),
                        pltpu.VMEM((CHUNK, CHUNK, SOLVE_UNITS), F32)],
        compiler_params=_params("parallel"),
        name="tri_inv",
    )(a_flat)


def _gdn_scan_kernel(qn_ref, kn_ref, vv_ref, z_ref, gc_ref, gr_ref, ti_ref, nw_ref, o_ref, s_scr, *, nchunk):
    @pl.when(pl.program_id(1) == 0)
    def _():
        s_scr[...] = jnp.zeros_like(s_scr)

    r, c = _tri_incl()
    incl = r >= c
    rep = GDN_V_HEADS // GDN_QK_HEADS
    dh = GDN_HEAD_DIM

    def chunk_body(ci, carry):
        rs = pl.ds(pl.multiple_of(ci * CHUNK, CHUNK), CHUNK)
        gc = gc_ref[0, rs, :]
        gr = gr_ref[0, ci]
        for hq in range(GDN_QK_HEADS):
            q = qn_ref[0, rs, hq * dh:(hq + 1) * dh]
            k = kn_ref[0, rs, hq * dh:(hq + 1) * dh]
            qk = _dot_nt(q.astype(BF16), k.astype(BF16))
            for e in range(rep):
                h = hq * rep + e
                hs = slice(h * dh, (h + 1) * dh)
                beta = gc[:, h:h + 1]
                gi = gc[:, GDN_V_HEADS + h:GDN_V_HEADS + h + 1]
                gj = gr[GDN_V_HEADS + h:GDN_V_HEADS + h + 1, :]
                glast = gi[CHUNK - 1:CHUNK, :]
                egi = jnp.exp(gi)
                attn = qk * jnp.exp(jnp.where(incl, gi - gj, -jnp.inf))
                v = vv_ref[0, rs, hs]
                rhs = jnp.concatenate([v * beta, k * (beta * egi)], 1).astype(BF16)
                sol = _dot(ti_ref[0, ci, h].astype(BF16), rhs)
                u, w = sol[:, :dh], sol[:, dh:]
                s = s_scr[h]
                sb = s.astype(BF16)
                v_new = u - _dot(w.astype(BF16), sb)
                vb = v_new.astype(BF16)
                o = _dot((q * egi).astype(BF16), sb) + _dot(attn.astype(BF16), vb)
                kt = (k * jnp.exp(glast - gi)).astype(BF16)
                s_scr[h] = s * jnp.exp(glast) + _dot_tn(kt, vb)
                o_ref[0, rs, hs] = (_rms(o, nw_ref[...]) * _silu(z_ref[0, rs, hs])).astype(BF16)
        return carry

    lax.fori_loop(0, nchunk, chunk_body, 0)


def _gdn_scan(qn, kn, vv, proj, gco, gro, tinv, norm_w, nchunk):
    b, seq_len, _ = qn.shape
    nc = seq_len // CHUNK
    rows = nchunk * CHUNK
    zblk = GDN_QKV_DIM // GDN_VAL_DIM
    return pl.pallas_call(
        functools.partial(_gdn_scan_kernel, nchunk=nchunk),
        grid=(b, nc // nchunk),
        in_specs=[
            pl.BlockSpec((1, rows, GDN_KEY_DIM), lambda bi, i: (bi, i, 0)),
            pl.BlockSpec((1, rows, GDN_KEY_DIM), lambda bi, i: (bi, i, 0)),
            pl.BlockSpec((1, rows, GDN_VAL_DIM), lambda bi, i: (bi, i, 0)),
            pl.BlockSpec((1, rows, GDN_VAL_DIM), lambda bi, i: (bi, i, zblk)),
            pl.BlockSpec((1, rows, LANES), lambda bi, i: (bi, i, 0)),
            pl.BlockSpec((1, nchunk, 2 * GDN_V_HEADS, CHUNK), lambda bi, i: (bi, i, 0, 0)),
            pl.BlockSpec((1, nchunk, GDN_V_HEADS, CHUNK, CHUNK), lambda bi, i: (bi, i, 0, 0, 0)),
            pl.BlockSpec((1, GDN_HEAD_DIM), lambda bi, i: (0, 0)),
        ],
        out_specs=pl.BlockSpec((1, rows, GDN_VAL_DIM), lambda bi, i: (bi, i, 0)),
        out_shape=jax.ShapeDtypeStruct((b, seq_len, GDN_VAL_DIM), BF16),
        scratch_shapes=[pltpu.VMEM((GDN_V_HEADS, GDN_HEAD_DIM, GDN_HEAD_DIM), F32)],
        compiler_params=_params("parallel", "arbitrary"),
        name="gdn_scan",
    )(qn, kn, vv, proj, gco, gro, tinv, norm_w)


def _gated_deltanet(a2d, b, seq_len, nw_in, w_in, conv_w, a_log, dt_bias, norm_w, nchunk):
    nc = seq_len // CHUNK
    ng = 2 * GDN_V_HEADS
    w_main = w_in[:, :GDN_MAIN].astype(BF16)
    w_gate = jnp.pad(w_in[:, GDN_MAIN:], ((0, 0), (0, LANES - ng))).astype(BF16)
    proj, gates = _inproj(a2d, nw_in, w_main, w_gate, 1024)
    proj = proj.reshape(b, seq_len, GDN_MAIN)
    gates = gates.reshape(b, seq_len, LANES)
    gates_r = jnp.swapaxes(gates[..., :ng].reshape(b, nc, CHUNK, ng), 2, 3)
    pad16 = jnp.zeros((GDN_V_HEADS,), F32)
    prm_row = jnp.zeros((SUBLANES, LANES), F32)
    prm_row = prm_row.at[0, :ng].set(jnp.concatenate([pad16, a_log]))
    prm_row = prm_row.at[1, :ng].set(jnp.concatenate([pad16, dt_bias]))
    prm_col = jnp.zeros((ng, LANES), F32)
    prm_col = prm_col.at[:, 0].set(jnp.concatenate([pad16, a_log]))
    prm_col = prm_col.at[:, 1].set(jnp.concatenate([pad16, dt_bias]))
    qn, kn, vv, gco, gro, a_mat = _gdn_prep(proj, gates, gates_r, conv_w, prm_row, prm_col, nchunk)
    units = b * nc * GDN_V_HEADS
    upad = -units % SOLVE_UNITS
    a_flat = jnp.pad(a_mat.reshape(units, CHUNK * CHUNK), ((0, upad), (0, 0)))
    tinv = _tri_inv(a_flat)[:units].reshape(b, nc, GDN_V_HEADS, CHUNK, CHUNK)
    o = _gdn_scan(qn, kn, vv, proj, gco, gro, tinv, norm_w.reshape(1, GDN_HEAD_DIM), nchunk)
    return o.reshape(b * seq_len, GDN_VAL_DIM)


def _mlstm_kernel(q_ref, k_ref, v_ref, og_ref, gc_ref, gr_ref, brow_ref, bcol_ref, nw_ref, o_ref,
                  c_scr, n_scr, m_scr, *, nchunk):
    i = pl.program_id(1)

    @pl.when(i == 0)
    def _():
        c_scr[...] = jnp.zeros_like(c_scr)
        n_scr[...] = jnp.zeros_like(n_scr)
        m_scr[...] = jnp.zeros_like(m_scr)

    r, c = _tri_incl()
    incl = r >= c
    tri = incl.astype(F32)
    tri_t = (r <= c).astype(F32)
    dk, dv = ML_QK_DIM, ML_V_DIM

    def chunk_body(ci, carry):
        rs = pl.ds(pl.multiple_of(ci * CHUNK, CHUNK), CHUNK)
        t = (i * nchunk + ci) * CHUNK + lax.broadcasted_iota(jnp.int32, (CHUNK, 1), 0)
        valid = (t >= N_PAD).astype(F32)
        gcap = ML_GATE_CAP * jnp.tanh((gc_ref[0, rs, :] + brow_ref[0:1, :]) / ML_GATE_CAP)
        bcum = _dot_f32(tri, jax.nn.log_sigmoid(gcap))
        gcap_r = ML_GATE_CAP * jnp.tanh((gr_ref[0, ci] + bcol_ref[:, 0:1]) / ML_GATE_CAP)
        bcum_r = _dot_f32(jax.nn.log_sigmoid(gcap_r), tri_t)
        for h in range(ML_HEADS):
            b_i = bcum[:, ML_HEADS + h:ML_HEADS + h + 1]
            b_j = bcum_r[ML_HEADS + h:ML_HEADS + h + 1, :]
            li_i = gcap[:, h:h + 1]
            li_j = gcap_r[h:h + 1, :]
            bl = b_i[CHUNK - 1:CHUNK, :]
            a_i = bl - b_i + li_i
            ms = m_scr[h]
            cs = c_scr[h]
            ns = n_scr[h]
            q = q_ref[0, rs, h * dk:(h + 1) * dk] * (dk ** -0.5)
            k = k_ref[0, rs, h * dk:(h + 1) * dk] * valid
            v = v_ref[0, rs, h * dv:(h + 1) * dv] * valid
            qb, kb, vb = q.astype(BF16), k.astype(BF16), v.astype(BF16)

            d_mat = jnp.where(incl, b_i - b_j + li_j, -jnp.inf)
            inter = b_i + ms
            m_t = jnp.maximum(jnp.max(d_mat, -1, keepdims=True), inter)
            w_d = jnp.exp(d_mat - m_t) * _dot_nt(qb, kb)
            sc = jnp.exp(inter - m_t)
            num = sc * _dot(qb, cs.astype(BF16)) + _dot(w_d.astype(BF16), vb)
            den = sc * jnp.sum(q * ns, -1, keepdims=True) + jnp.sum(w_d, -1, keepdims=True)
            hh = num / jnp.maximum(jnp.abs(den), jnp.exp(-m_t))

            m_new = jnp.maximum(bl + ms, jnp.max(a_i, 0, keepdims=True))
            dec = jnp.exp(bl + ms - m_new)
            kw = k * jnp.exp(a_i - m_new)
            c_scr[h] = dec * cs + _dot_tn(kw.astype(BF16), vb)
            n_scr[h] = dec * ns + jnp.sum(kw, 0, keepdims=True)
            m_scr[h] = m_new

            vs = slice(h * dv, (h + 1) * dv)
            o_ref[0, rs, vs] = (_rms(hh, nw_ref[:, vs]) * jax.nn.sigmoid(og_ref[0, rs, vs])).astype(BF16)
        return carry

    lax.fori_loop(0, nchunk, chunk_body, 0)


def _mlstm(a2d, b, seq_len, nw_in, w_in, gate_b, norm_w, nchunk):
    nc = seq_len // CHUNK
    ng = 2 * ML_HEADS
    w_main = w_in[:, :ML_MAIN].astype(BF16)
    w_gate = jnp.pad(w_in[:, ML_MAIN:], ((0, 0), (0, LANES - ng))).astype(BF16)
    proj, gates = _inproj(a2d, nw_in, w_main, w_gate, 1024)
    proj = proj.reshape(b, seq_len, ML_MAIN)
    gates = gates.reshape(b, seq_len, LANES)
    gates_r = jnp.swapaxes(gates[..., :ng].reshape(b, nc, CHUNK, ng), 2, 3)
    b_row = jnp.zeros((SUBLANES, LANES), F32).at[0, :ng].set(gate_b)
    b_col = jnp.zeros((ng, LANES), F32).at[:, 0].set(gate_b)
    rows = nchunk * CHUNK
    qkb = ML_QK_TOT
    o = pl.pallas_call(
        functools.partial(_mlstm_kernel, nchunk=nchunk),
        grid=(b, nc // nchunk),
        in_specs=[
            pl.BlockSpec((1, rows, qkb), lambda bi, i: (bi, i, 0)),
            pl.BlockSpec((1, rows, qkb), lambda bi, i: (bi, i, 1)),
            pl.BlockSpec((1, rows, ML_V_TOT), lambda bi, i: (bi, i, 1)),
            pl.BlockSpec((1, rows, ML_V_TOT), lambda bi, i: (bi, i, 2)),
            pl.BlockSpec((1, rows, LANES), lambda bi, i: (bi, i, 0)),
            pl.BlockSpec((1, nchunk, ng, CHUNK), lambda bi, i: (bi, i, 0, 0)),
            pl.BlockSpec((SUBLANES, LANES), lambda bi, i: (0, 0)),
            pl.BlockSpec((ng, LANES), lambda bi, i: (0, 0)),
            pl.BlockSpec((1, ML_V_TOT), lambda bi, i: (0, 0)),
        ],
        out_specs=pl.BlockSpec((1, rows, ML_V_TOT), lambda bi, i: (bi, i, 0)),
        out_shape=jax.ShapeDtypeStruct((b, seq_len, ML_V_TOT), BF16),
        scratch_shapes=[pltpu.VMEM((ML_HEADS, ML_QK_DIM, ML_V_DIM), F32),
                        pltpu.VMEM((ML_HEADS, 1, ML_QK_DIM), F32),
                        pltpu.VMEM((ML_HEADS, 1, 1), F32)],
        compiler_params=_params("parallel", "arbitrary"),
        name="mlstm",
    )(proj, proj, proj, proj, gates, gates_r, b_row, b_col, norm_w.reshape(1, ML_V_TOT))
    return o.reshape(b * seq_len, ML_V_TOT)


def _ffn_weights(w_up, conv_w, conv_b, w_down):
    f = w_down.shape[0]
    nblk = f // FFN_BLOCK

    def regroup(t):
        g = t[..., :f].reshape(t.shape[:-1] + (nblk, FFN_BLOCK))
        u = t[..., f:].reshape(t.shape[:-1] + (nblk, FFN_BLOCK))
        return jnp.moveaxis(jnp.concatenate([g, u], -1), -2, 0)

    return (regroup(w_up).astype(BF16), regroup(conv_w), regroup(conv_b[None]),
            w_down.reshape(nblk, FFN_BLOCK, -1).astype(BF16))


def kernel(x, meta_tokens, norm_w, gdn_w_in, gdn_conv_w, gdn_a_log, gdn_dt_bias, gdn_norm_w, gdn_w_out,
           ml_w_in, ml_gate_b, ml_norm_w, ml_w_out, ffn_w_up, ffn_conv_w, ffn_conv_b, ffn_w_down):
    b, s, d = x.shape
    depth = norm_w.shape[0]
    pad = jnp.zeros((b, N_PAD, d), x.dtype)
    meta = jnp.broadcast_to(meta_tokens[None].astype(x.dtype), (b, N_META, d))
    hs = jnp.concatenate([pad, meta, x], 1)
    seq_len = hs.shape[1]
    nc = seq_len // CHUNK
    nchunk = 3 if nc % 3 == 0 else 1
    for i in range(depth):
        j = i // 2
        nw = norm_w[i].reshape(4, 1, d)
        hs2d = hs.reshape(b * seq_len, d)
        if i % 2 == 0:
            mix = _gated_deltanet(hs2d, b, seq_len, nw[0], gdn_w_in[j], gdn_conv_w[j], gdn_a_log[j],
                                  gdn_dt_bias[j], gdn_norm_w[j], nchunk)
            w_out = gdn_w_out[j]
        else:
            mix = _mlstm(hs2d, b, seq_len, nw[0], ml_w_in[j], ml_gate_b[j], ml_norm_w[j], nchunk)
            w_out = ml_w_out[j]
        hs = _outproj(mix, w_out.astype(BF16), nw[1], hs2d, seq_len).reshape(b, seq_len, d)
        hs = _ffn(hs, nw[2], *_ffn_weights(ffn_w_up[i], ffn_conv_w[i], ffn_conv_b[i], ffn_w_down[i]), nw[3])
    return hs[:, N_PAD + N_META:]
```

```python
import functools

import jax
import jax.numpy as jnp
from jax import lax
from jax.experimental import pallas as pl
from jax.experimental.pallas import tpu as pltpu

F32 = jnp.float32
BF16 = jnp.bfloat16

CHUNK = 64
N_META = 16
N_PAD = CHUNK - N_META
RMS_EPS = 1e-6
L2_EPS = 1e-6

GDN_QK_HEADS = 8
GDN_V_HEADS = 16
GDN_HEAD_DIM = 128
GDN_CONV = 4
GDN_KEY_DIM = GDN_QK_HEADS * GDN_HEAD_DIM
GDN_VAL_DIM = GDN_V_HEADS * GDN_HEAD_DIM
GDN_QKV_DIM = 2 * GDN_KEY_DIM + GDN_VAL_DIM
GDN_MAIN = GDN_QKV_DIM + GDN_VAL_DIM

ML_HEADS = 4
ML_QK_DIM = 128
ML_V_DIM = 256
ML_GATE_CAP = 15.0
ML_QK_TOT = ML_HEADS * ML_QK_DIM
ML_V_TOT = ML_HEADS * ML_V_DIM
ML_MAIN = 2 * ML_QK_TOT + 2 * ML_V_TOT

FFN_CONV = 3
FFN_BLOCK = 256

LANES = 128
SUBLANES = 8
SOLVE_UNITS = 256
VMEM_LIMIT = 56 * 1024 * 1024


def _largest_tile(n, cap):
    d = (min(cap, n) // SUBLANES) * SUBLANES
    while d > SUBLANES and n % d:
        d -= SUBLANES
    assert d >= SUBLANES and n % d == 0, (n, cap)
    return d


def _params(*sem):
    return pltpu.CompilerParams(dimension_semantics=sem, vmem_limit_bytes=VMEM_LIMIT)


def _rms(x, w):
    return x * lax.rsqrt(jnp.mean(x * x, -1, keepdims=True) + RMS_EPS) * w


def _silu(x):
    return x * jax.nn.sigmoid(x)


def _dot(a, b):
    return jnp.dot(a, b, preferred_element_type=F32)


def _dot_nt(a, b):
    return lax.dot_general(a, b, (((1,), (1,)), ((), ())), preferred_element_type=F32)


def _dot_tn(a, b):
    return lax.dot_general(a, b, (((0,), (0,)), ((), ())), preferred_element_type=F32)


def _dot_f32(a, b):
    return jnp.dot(a, b, preferred_element_type=F32, precision=lax.Precision.HIGHEST)


def _tri_incl():
    r = lax.broadcasted_iota(jnp.int32, (CHUNK, CHUNK), 0)
    c = lax.broadcasted_iota(jnp.int32, (CHUNK, CHUNK), 1)
    return r, c


def _inproj_kernel(x_ref, nw_ref, wm_ref, wg_ref, om_ref, og_ref, a_scr):
    @pl.when(pl.program_id(1) == 0)
    def _():
        a = _rms(x_ref[...], nw_ref[...]).astype(BF16)
        a_scr[...] = a
        og_ref[...] = _dot(a, wg_ref[...])

    om_ref[...] = _dot(a_scr[...], wm_ref[...])


def _inproj(x2d, nw, w_main, w_gate, tn):
    m, d = x2d.shape
    n = w_main.shape[1]
    tm = _largest_tile(m, 1400)
    return pl.pallas_call(
        _inproj_kernel,
        grid=(m // tm, n // tn),
        in_specs=[
            pl.BlockSpec((tm, d), lambda i, j: (i, 0)),
            pl.BlockSpec((1, d), lambda i, j: (0, 0)),
            pl.BlockSpec((d, tn), lambda i, j: (0, j)),
            pl.BlockSpec((d, LANES), lambda i, j: (0, 0)),
        ],
        out_specs=[
            pl.BlockSpec((tm, tn), lambda i, j: (i, j)),
            pl.BlockSpec((tm, LANES), lambda i, j: (i, 0)),
        ],
        out_shape=[jax.ShapeDtypeStruct((m, n), F32), jax.ShapeDtypeStruct((m, LANES), F32)],
        scratch_shapes=[pltpu.VMEM((tm, d), BF16)],
        compiler_params=_params("parallel", "arbitrary"),
        name="inproj",
    )(x2d, nw, w_main, w_gate)


def _outproj_kernel(o_ref, w_ref, nw_ref, hs_ref, out_ref, *, tm, seq_len):
    y = _rms(_dot(o_ref[...], w_ref[...]), nw_ref[...])
    t = (pl.program_id(0) * tm) % seq_len + lax.broadcasted_iota(jnp.int32, (tm, 1), 0)
    out_ref[...] = hs_ref[...] + jnp.where(t >= N_PAD, y, 0.0)


def _outproj(o2d, w_out, nw, hs2d, seq_len):
    m, kin = o2d.shape
    d = hs2d.shape[1]
    tm = _largest_tile(seq_len, 1400)
    return pl.pallas_call(
        functools.partial(_outproj_kernel, tm=tm, seq_len=seq_len),
        grid=(m // tm,),
        in_specs=[
            pl.BlockSpec((tm, kin), lambda i: (i, 0)),
            pl.BlockSpec((kin, d), lambda i: (0, 0)),
            pl.BlockSpec((1, d), lambda i: (0, 0)),
            pl.BlockSpec((tm, d), lambda i: (i, 0)),
        ],
        out_specs=pl.BlockSpec((tm, d), lambda i: (i, 0)),
        out_shape=jax.ShapeDtypeStruct((m, d), F32),
        compiler_params=_params("parallel"),
        name="outproj",
    )(o2d, w_out, nw, hs2d)


def _ffn_kernel(h_ref, halo_ref, nw2_ref, wup_ref, cw_ref, cb_ref, wdn_ref, nw3_ref, out_ref, *, tm, nblk):
    i = pl.program_id(1)
    h = h_ref[0]
    halo = jnp.where(i == 0, 0.0, halo_ref[0])
    a = _rms(jnp.concatenate([halo, h], 0), nw2_ref[...]).astype(BF16)

    def body(j, acc):
        u = _dot(a, wup_ref[j])
        cw = cw_ref[j]
        y = (cw[2:3] * u[SUBLANES:]
             + cw[1:2] * pltpu.roll(u, 1, 0)[SUBLANES:]
             + cw[0:1] * pltpu.roll(u, 2, 0)[SUBLANES:]
             + cb_ref[j])
        act = (_silu(y[:, :FFN_BLOCK]) * y[:, FFN_BLOCK:]).astype(BF16)
        return acc + _dot(act, wdn_ref[j])

    f = lax.fori_loop(0, nblk, body, jnp.zeros((tm, h.shape[1]), F32))
    t = i * tm + lax.broadcasted_iota(jnp.int32, (tm, 1), 0)
    out_ref[0] = h + jnp.where(t >= N_PAD, _rms(f, nw3_ref[...]), 0.0)


def _ffn(hs, nw2, wup_r, cw_r, cb_r, wdn_r, nw3):
    b, seq_len, d = hs.shape
    nblk = wup_r.shape[0]
    tm = _largest_tile(seq_len, 700)
    hb = tm // SUBLANES
    return pl.pallas_call(
        functools.partial(_ffn_kernel, tm=tm, nblk=nblk),
        grid=(b, seq_len // tm),
        in_specs=[
            pl.BlockSpec((1, tm, d), lambda bi, i: (bi, i, 0)),
            pl.BlockSpec((1, SUBLANES, d), lambda bi, i: (bi, jnp.maximum(i * hb - 1, 0), 0)),
            pl.BlockSpec((1, d), lambda bi, i: (0, 0)),
            pl.BlockSpec(wup_r.shape, lambda bi, i: (0, 0, 0)),
            pl.BlockSpec(cw_r.shape, lambda bi, i: (0, 0, 0)),
            pl.BlockSpec(cb_r.shape, lambda bi, i: (0, 0, 0)),
            pl.BlockSpec(wdn_r.shape, lambda bi, i: (0, 0, 0)),
            pl.BlockSpec((1, d), lambda bi, i: (0, 0)),
        ],
        out_specs=pl.BlockSpec((1, tm, d), lambda bi, i: (bi, i, 0)),
        out_shape=jax.ShapeDtypeStruct(hs.shape, F32),
        compiler_params=_params("parallel", "parallel"),
        name="convffn",
    )(hs, hs, nw2, wup_r, cw_r, cb_r, wdn_r, nw3)


def _gdn_gates_col(graw, prm_ref):
    beta = jax.nn.sigmoid(graw)
    g = -jnp.exp(prm_ref[0:1, :]) * jax.nn.softplus(graw + prm_ref[1:2, :])
    return beta, g


def _gdn_prep_kernel(qk_ref, v_ref, qkh_ref, vh_ref, cwqk_ref, cwv_ref, gc_ref, gr_ref, prow_ref, pcol_ref,
                     qn_ref, kn_ref, vv_ref, gco_ref, gro_ref, a_ref, *, nchunk):
    i = pl.program_id(1)
    rows = nchunk * CHUNK
    t = i * rows + lax.broadcasted_iota(jnp.int32, (rows, 1), 0)
    valid = (t >= N_PAD).astype(F32)
    first = i == 0

    def conv_silu(x_ref, halo_ref, cw_ref, col):
        sl = slice(col * GDN_HEAD_DIM, (col + 1) * GDN_HEAD_DIM)
        halo = jnp.where(first, 0.0, halo_ref[0, :, sl])
        x = jnp.concatenate([halo, x_ref[0, :, sl]], 0)
        cw = cw_ref[:, sl]
        y = (cw[3:4] * x[SUBLANES:]
             + cw[2:3] * pltpu.roll(x, 1, 0)[SUBLANES:]
             + cw[1:2] * pltpu.roll(x, 2, 0)[SUBLANES:]
             + cw[0:1] * pltpu.roll(x, 3, 0)[SUBLANES:])
        return _silu(y)

    def l2n(x):
        return x * lax.rsqrt(jnp.sum(x * x, -1, keepdims=True) + L2_EPS)

    for hq in range(GDN_QK_HEADS):
        sl = slice(hq * GDN_HEAD_DIM, (hq + 1) * GDN_HEAD_DIM)
        qn_ref[0, :, sl] = l2n(conv_silu(qk_ref, qkh_ref, cwqk_ref, hq)) * (GDN_HEAD_DIM ** -0.5)
        kn_ref[0, :, sl] = l2n(conv_silu(qk_ref, qkh_ref, cwqk_ref, GDN_QK_HEADS + hq)) * valid
    for h in range(GDN_V_HEADS):
        sl = slice(h * GDN_HEAD_DIM, (h + 1) * GDN_HEAD_DIM)
        vv_ref[0, :, sl] = conv_silu(v_ref, vh_ref, cwv_ref, h) * valid

    r, c = _tri_incl()
    tri = (r >= c).astype(F32)
    tri_t = (r <= c).astype(F32)
    incl = r >= c
    strict = r > c
    rep = GDN_V_HEADS // GDN_QK_HEADS
    for ci in range(nchunk):
        rs = slice(ci * CHUNK, (ci + 1) * CHUNK)
        beta, g = _gdn_gates_col(gc_ref[0, rs, :], prow_ref)
        gcum = _dot_f32(tri, g)
        lane = lax.broadcasted_iota(jnp.int32, (CHUNK, LANES), 1)
        gco_ref[0, rs, :] = jnp.where(lane < GDN_V_HEADS, beta, gcum)
        graw_r = gr_ref[0, ci]
        g_r = -jnp.exp(pcol_ref[:, 0:1]) * jax.nn.softplus(graw_r + pcol_ref[:, 1:2])
        gcum_r = _dot_f32(g_r, tri_t)
        gro_ref[0, ci] = gcum_r
        for hq in range(GDN_QK_HEADS):
            kc = kn_ref[0, rs, hq * GDN_HEAD_DIM:(hq + 1) * GDN_HEAD_DIM].astype(BF16)
            kk = _dot_nt(kc, kc)
            for e in range(rep):
                h = hq * rep + e
                gi = gcum[:, GDN_V_HEADS + h:GDN_V_HEADS + h + 1]
                gj = gcum_r[GDN_V_HEADS + h:GDN_V_HEADS + h + 1, :]
                dec = jnp.exp(jnp.where(incl, gi - gj, -jnp.inf))
                a_ref[0, ci, h] = jnp.where(strict, kk * beta[:, h:h + 1] * dec, 0.0)


def _gdn_prep(proj, gates, gates_r, conv_w, prm_row, prm_col, nchunk):
    b, seq_len, _ = proj.shape
    nc = seq_len // CHUNK
    rows = nchunk * CHUNK
    hb = rows // SUBLANES
    half = GDN_QKV_DIM // 2
    halo_map0 = lambda bi, i: (bi, jnp.maximum(i * hb - 1, 0), 0)
    halo_map1 = lambda bi, i: (bi, jnp.maximum(i * hb - 1, 0), 1)
    return pl.pallas_call(
        functools.partial(_gdn_prep_kernel, nchunk=nchunk),
        grid=(b, nc // nchunk),
        in_specs=[
            pl.BlockSpec((1, rows, half), lambda bi, i: (bi, i, 0)),
            pl.BlockSpec((1, rows, half), lambda bi, i: (bi, i, 1)),
            pl.BlockSpec((1, SUBLANES, half), halo_map0),
            pl.BlockSpec((1, SUBLANES, half), halo_map1),
            pl.BlockSpec((GDN_CONV, half), lambda bi, i: (0, 0)),
            pl.BlockSpec((GDN_CONV, half), lambda bi, i: (0, 1)),
            pl.BlockSpec((1, rows, LANES), lambda bi, i: (bi, i, 0)),
            pl.BlockSpec((1, nchunk, 2 * GDN_V_HEADS, CHUNK), lambda bi, i: (bi, i, 0, 0)),
            pl.BlockSpec((SUBLANES, LANES), lambda bi, i: (0, 0)),
            pl.BlockSpec((2 * GDN_V_HEADS, LANES), lambda bi, i: (0, 0)),
        ],
        out_specs=[
            pl.BlockSpec((1, rows, GDN_KEY_DIM), lambda bi, i: (bi, i, 0)),
            pl.BlockSpec((1, rows, GDN_KEY_DIM), lambda bi, i: (bi, i, 0)),
            pl.BlockSpec((1, rows, GDN_VAL_DIM), lambda bi, i: (bi, i, 0)),
            pl.BlockSpec((1, rows, LANES), lambda bi, i: (bi, i, 0)),
            pl.BlockSpec((1, nchunk, 2 * GDN_V_HEADS, CHUNK), lambda bi, i: (bi, i, 0, 0)),
            pl.BlockSpec((1, nchunk, GDN_V_HEADS, CHUNK, CHUNK), lambda bi, i: (bi, i, 0, 0, 0)),
        ],
        out_shape=[
            jax.ShapeDtypeStruct((b, seq_len, GDN_KEY_DIM), F32),
            jax.ShapeDtypeStruct((b, seq_len, GDN_KEY_DIM), F32),
            jax.ShapeDtypeStruct((b, seq_len, GDN_VAL_DIM), F32),
            jax.ShapeDtypeStruct((b, seq_len, LANES), F32),
            jax.ShapeDtypeStruct((b, nc, 2 * GDN_V_HEADS, CHUNK), F32),
            jax.ShapeDtypeStruct((b, nc, GDN_V_HEADS, CHUNK, CHUNK), F32),
        ],
        compiler_params=_params("parallel", "parallel"),
        name="gdn_prep",
    )(proj, proj, proj, proj, conv_w, conv_w, gates, gates_r, prm_row, prm_col)


def _tri_inv_kernel(a_ref, x_ref, at_scr, xt_scr):
    per = LANES // CHUNK
    for cb in range(CHUNK // per):
        tile = a_ref[:, cb * LANES:(cb + 1) * LANES].T
        for p in range(per):
            at_scr[cb * per + p] = tile[p * CHUNK:(p + 1) * CHUNK]

    xt_scr[...] = jnp.zeros_like(xt_scr)
    jrow = lax.broadcasted_iota(jnp.int32, (SUBLANES, SOLVE_UNITS), 0)
    sub = SUBLANES
    for ib in range(CHUNK // sub):
        def row_body(ii, carry, ib=ib):
            i = ib * sub + ii
            acc = [(jrow + jb * sub == i).astype(F32) for jb in range(ib + 1)]
            for kb in range(ib + 1):
                for kk in range(sub):
                    k = kb * sub + kk
                    a = at_scr[i, k:k + 1, :]
                    for jb in range(kb + 1):
                        acc[jb] = acc[jb] - a * xt_scr[k, jb * sub:(jb + 1) * sub, :]
            for jb in range(ib + 1):
                xt_scr[i, jb * sub:(jb + 1) * sub, :] = acc[jb]
            return carry

        lax.fori_loop(0, sub, row_body, 0)

    for cb in range(CHUNK // per):
        tile = jnp.concatenate([xt_scr[cb * per + p] for p in range(per)], 0)
        x_ref[:, cb * LANES:(cb + 1) * LANES] = tile.T


def _tri_inv(a_flat):
    u, e = a_flat.shape
    return pl.pallas_call(
        _tri_inv_kernel,
        grid=(pl.cdiv(u, SOLVE_UNITS),),
        in_specs=[pl.BlockSpec((SOLVE_UNITS, e), lambda i: (i, 0))],
        out_specs=pl.BlockSpec((SOLVE_UNITS, e), lambda i: (i, 0)),
        out_shape=jax.ShapeDtypeStruct((u, e), F32),
        scratch_shapes=[pltpu.VMEM((CHUNK, CHUNK, SOLVE_UNITS), F32),
                        pltpu.VMEM((CHUNK, CHUNK, SOLVE_UNITS), F32)],
        compiler_params=_params("parallel"),
        name="tri_inv",
    )(a_flat)


def _gdn_scan_kernel(qn_ref, kn_ref, vv_ref, z_ref, gc_ref, gr_ref, ti_ref, nw_ref, o_ref, s_scr, *, nchunk):
    @pl.when(pl.program_id(1) == 0)
    def _():
        s_scr[...] = jnp.zeros_like(s_scr)

    r, c = _tri_incl()
    incl = r >= c
    rep = GDN_V_HEADS // GDN_QK_HEADS
    dh = GDN_HEAD_DIM

    def chunk_body(ci, carry):
        rs = pl.ds(pl.multiple_of(ci * CHUNK, CHUNK), CHUNK)
        gc = gc_ref[0, rs, :]
        gr = gr_ref[0, ci]
        pre = []
        for hq in range(GDN_QK_HEADS):
            q = qn_ref[0, rs, hq * dh:(hq + 1) * dh]
            k = kn_ref[0, rs, hq * dh:(hq + 1) * dh]
            qk = _dot_nt(q.astype(BF16), k.astype(BF16))
            for e in range(rep):
                h = hq * rep + e
                beta = gc[:, h:h + 1]
                gi = gc[:, GDN_V_HEADS + h:GDN_V_HEADS + h + 1]
                gj = gr[GDN_V_HEADS + h:GDN_V_HEADS + h + 1, :]
                glast = gi[CHUNK - 1:CHUNK, :]
                egi = jnp.exp(gi)
                attn = (qk * jnp.exp(jnp.where(incl, gi - gj, -jnp.inf))).astype(BF16)
                v = vv_ref[0, rs, h * dh:(h + 1) * dh]
                rhs = jnp.concatenate([v * beta, k * (beta * egi)], 1).astype(BF16)
                sol = _dot(ti_ref[0, ci, h].astype(BF16), rhs)
                u, w = sol[:, :dh], sol[:, dh:]
                wq = jnp.concatenate([w, q * egi], 0).astype(BF16)
                kt = (k * jnp.exp(glast - gi)).astype(BF16)
                pre.append((u, wq, attn, kt, jnp.exp(glast)))
        mid = []
        for h in range(GDN_V_HEADS):
            u, wq, _, _, _ = pre[h]
            ws_qs = _dot(wq, s_scr[h].astype(BF16))
            mid.append((ws_qs[CHUNK:], (u - ws_qs[:CHUNK]).astype(BF16)))
        for h in range(GDN_V_HEADS):
            _, _, attn, kt, gt = pre[h]
            qs, vb = mid[h]
            hs = slice(h * dh, (h + 1) * dh)
            o = qs + _dot(attn, vb)
            s_scr[h] = s_scr[h] * gt + _dot_tn(kt, vb)
            o_ref[0, rs, hs] = (_rms(o, nw_ref[...]) * _silu(z_ref[0, rs, hs])).astype(BF16)
        return carry

    lax.fori_loop(0, nchunk, chunk_body, 0)


def _gdn_scan(qn, kn, vv, proj, gco, gro, tinv, norm_w, nchunk):
    b, seq_len, _ = qn.shape
    nc = seq_len // CHUNK
    rows = nchunk * CHUNK
    zblk = GDN_QKV_DIM // GDN_VAL_DIM
    return pl.pallas_call(
        functools.partial(_gdn_scan_kernel, nchunk=nchunk),
        grid=(b, nc // nchunk),
        in_specs=[
            pl.BlockSpec((1, rows, GDN_KEY_DIM), lambda bi, i: (bi, i, 0)),
            pl.BlockSpec((1, rows, GDN_KEY_DIM), lambda bi, i: (bi, i, 0)),
            pl.BlockSpec((1, rows, GDN_VAL_DIM), lambda bi, i: (bi, i, 0)),
            pl.BlockSpec((1, rows, GDN_VAL_DIM), lambda bi, i: (bi, i, zblk)),
            pl.BlockSpec((1, rows, LANES), lambda bi, i: (bi, i, 0)),
            pl.BlockSpec((1, nchunk, 2 * GDN_V_HEADS, CHUNK), lambda bi, i: (bi, i, 0, 0)),
            pl.BlockSpec((1, nchunk, GDN_V_HEADS, CHUNK, CHUNK), lambda bi, i: (bi, i, 0, 0, 0)),
            pl.BlockSpec((1, GDN_HEAD_DIM), lambda bi, i: (0, 0)),
        ],
        out_specs=pl.BlockSpec((1, rows, GDN_VAL_DIM), lambda bi, i: (bi, i, 0)),
        out_shape=jax.ShapeDtypeStruct((b, seq_len, GDN_VAL_DIM), BF16),
        scratch_shapes=[pltpu.VMEM((GDN_V_HEADS, GDN_HEAD_DIM, GDN_HEAD_DIM), F32)],
        compiler_params=_params("parallel", "arbitrary"),
        name="gdn_scan",
    )(qn, kn, vv, proj, gco, gro, tinv, norm_w)


def _gated_deltanet(a2d, b, seq_len, nw_in, w_in, conv_w, a_log, dt_bias, norm_w, nchunk):
    nc = seq_len // CHUNK
    ng = 2 * GDN_V_HEADS
    w_main = w_in[:, :GDN_MAIN].astype(BF16)
    w_gate = jnp.pad(w_in[:, GDN_MAIN:], ((0, 0), (0, LANES - ng))).astype(BF16)
    proj, gates = _inproj(a2d, nw_in, w_main, w_gate, 1024)
    proj = proj.reshape(b, seq_len, GDN_MAIN)
    gates = gates.reshape(b, seq_len, LANES)
    gates_r = jnp.swapaxes(gates[..., :ng].reshape(b, nc, CHUNK, ng), 2, 3)
    pad16 = jnp.zeros((GDN_V_HEADS,), F32)
    prm_row = jnp.zeros((SUBLANES, LANES), F32)
    prm_row = prm_row.at[0, :ng].set(jnp.concatenate([pad16, a_log]))
    prm_row = prm_row.at[1, :ng].set(jnp.concatenate([pad16, dt_bias]))
    prm_col = jnp.zeros((ng, LANES), F32)
    prm_col = prm_col.at[:, 0].set(jnp.concatenate([pad16, a_log]))
    prm_col = prm_col.at[:, 1].set(jnp.concatenate([pad16, dt_bias]))
    qn, kn, vv, gco, gro, a_mat = _gdn_prep(proj, gates, gates_r, conv_w, prm_row, prm_col, nchunk)
    units = b * nc * GDN_V_HEADS
    tinv = _tri_inv(a_mat.reshape(units, CHUNK * CHUNK)).reshape(b, nc, GDN_V_HEADS, CHUNK, CHUNK)
    o = _gdn_scan(qn, kn, vv, proj, gco, gro, tinv, norm_w.reshape(1, GDN_HEAD_DIM), nchunk)
    return o.reshape(b * seq_len, GDN_VAL_DIM)


def _mlstm_kernel(q_ref, k_ref, v_ref, og_ref, gc_ref, gr_ref, brow_ref, bcol_ref, nw_ref, o_ref,
                  c_scr, n_scr, m_scr, *, nchunk):
    i = pl.program_id(1)

    @pl.when(i == 0)
    def _():
        c_scr[...] = jnp.zeros_like(c_scr)
        n_scr[...] = jnp.zeros_like(n_scr)
        m_scr[...] = jnp.zeros_like(m_scr)

    r, c = _tri_incl()
    incl = r >= c
    tri = incl.astype(F32)
    tri_t = (r <= c).astype(F32)
    dk, dv = ML_QK_DIM, ML_V_DIM

    def chunk_body(ci, carry):
        rs = pl.ds(pl.multiple_of(ci * CHUNK, CHUNK), CHUNK)
        t = (i * nchunk + ci) * CHUNK + lax.broadcasted_iota(jnp.int32, (CHUNK, 1), 0)
        valid = (t >= N_PAD).astype(F32)
        gcap = ML_GATE_CAP * jnp.tanh((gc_ref[0, rs, :] + brow_ref[0:1, :]) / ML_GATE_CAP)
        bcum = _dot_f32(tri, jax.nn.log_sigmoid(gcap))
        gcap_r = ML_GATE_CAP * jnp.tanh((gr_ref[0, ci] + bcol_ref[:, 0:1]) / ML_GATE_CAP)
        bcum_r = _dot_f32(jax.nn.log_sigmoid(gcap_r), tri_t)
        pre = []
        for h in range(ML_HEADS):
            b_i = bcum[:, ML_HEADS + h:ML_HEADS + h + 1]
            b_j = bcum_r[ML_HEADS + h:ML_HEADS + h + 1, :]
            li_i = gcap[:, h:h + 1]
            li_j = gcap_r[h:h + 1, :]
            bl = b_i[CHUNK - 1:CHUNK, :]
            a_i = bl - b_i + li_i
            ms = m_scr[h]
            q = q_ref[0, rs, h * dk:(h + 1) * dk] * (dk ** -0.5)
            k = k_ref[0, rs, h * dk:(h + 1) * dk] * valid
            v = v_ref[0, rs, h * dv:(h + 1) * dv] * valid
            qb, kb, vb = q.astype(BF16), k.astype(BF16), v.astype(BF16)
            qk = _dot_nt(qb, kb)
            m_new = jnp.maximum(bl + ms, jnp.max(a_i, 0, keepdims=True))
            dec = jnp.exp(bl + ms - m_new)
            kw = k * jnp.exp(a_i - m_new)
            kv = _dot_tn(kw.astype(BF16), vb)
            d_mat = jnp.where(incl, b_i - b_j + li_j, -jnp.inf)
            inter = b_i + ms
            m_t = jnp.maximum(jnp.max(d_mat, -1, keepdims=True), inter)
            pre.append((q, qb, vb, qk, kv, kw, d_mat, inter, m_t, m_new, dec))
        mid = []
        for h in range(ML_HEADS):
            q, qb, vb, qk, kv, kw, d_mat, inter, m_t, m_new, dec = pre[h]
            w_d = jnp.exp(d_mat - m_t) * qk
            mid.append((_dot(w_d.astype(BF16), vb), jnp.sum(w_d, -1, keepdims=True)))
        for h in range(ML_HEADS):
            q, qb, vb, qk, kv, kw, d_mat, inter, m_t, m_new, dec = pre[h]
            wv, wsum = mid[h]
            cs = c_scr[h]
            ns = n_scr[h]
            sc = jnp.exp(inter - m_t)
            num = sc * _dot(qb, cs.astype(BF16)) + wv
            den = sc * jnp.sum(q * ns, -1, keepdims=True) + wsum
            hh = num / jnp.maximum(jnp.abs(den), jnp.exp(-m_t))
            c_scr[h] = dec * cs + kv
            n_scr[h] = dec * ns + jnp.sum(kw, 0, keepdims=True)
            m_scr[h] = m_new
            vs = slice(h * dv, (h + 1) * dv)
            o_ref[0, rs, vs] = (_rms(hh, nw_ref[:, vs]) * jax.nn.sigmoid(og_ref[0, rs, vs])).astype(BF16)
        return carry

    lax.fori_loop(0, nchunk, chunk_body, 0)


def _mlstm(a2d, b, seq_len, nw_in, w_in, gate_b, norm_w, nchunk):
    nc = seq_len // CHUNK
    ng = 2 * ML_HEADS
    w_main = w_in[:, :ML_MAIN].astype(BF16)
    w_gate = jnp.pad(w_in[:, ML_MAIN:], ((0, 0), (0, LANES - ng))).astype(BF16)
    proj, gates = _inproj(a2d, nw_in, w_main, w_gate, 1024)
    proj = proj.reshape(b, seq_len, ML_MAIN)
    gates = gates.reshape(b, seq_len, LANES)
    gates_r = jnp.swapaxes(gates[..., :ng].reshape(b, nc, CHUNK, ng), 2, 3)
    b_row = jnp.zeros((SUBLANES, LANES), F32).at[0, :ng].set(gate_b)
    b_col = jnp.zeros((ng, LANES), F32).at[:, 0].set(gate_b)
    rows = nchunk * CHUNK
    qkb = ML_QK_TOT
    o = pl.pallas_call(
        functools.partial(_mlstm_kernel, nchunk=nchunk),
        grid=(b, nc // nchunk),
        in_specs=[
            pl.BlockSpec((1, rows, qkb), lambda bi, i: (bi, i, 0)),
            pl.BlockSpec((1, rows, qkb), lambda bi, i: (bi, i, 1)),
            pl.BlockSpec((1, rows, ML_V_TOT), lambda bi, i: (bi, i, 1)),
            pl.BlockSpec((1, rows, ML_V_TOT), lambda bi, i: (bi, i, 2)),
            pl.BlockSpec((1, rows, LANES), lambda bi, i: (bi, i, 0)),
            pl.BlockSpec((1, nchunk, ng, CHUNK), lambda bi, i: (bi, i, 0, 0)),
            pl.BlockSpec((SUBLANES, LANES), lambda bi, i: (0, 0)),
            pl.BlockSpec((ng, LANES), lambda bi, i: (0, 0)),
            pl.BlockSpec((1, ML_V_TOT), lambda bi, i: (0, 0)),
        ],
        out_specs=pl.BlockSpec((1, rows, ML_V_TOT), lambda bi, i: (bi, i, 0)),
        out_shape=jax.ShapeDtypeStruct((b, seq_len, ML_V_TOT), BF16),
        scratch_shapes=[pltpu.VMEM((ML_HEADS, ML_QK_DIM, ML_V_DIM), F32),
                        pltpu.VMEM((ML_HEADS, 1, ML_QK_DIM), F32),
                        pltpu.VMEM((ML_HEADS, 1, 1), F32)],
        compiler_params=_params("parallel", "arbitrary"),
        name="mlstm",
    )(proj, proj, proj, proj, gates, gates_r, b_row, b_col, norm_w.reshape(1, ML_V_TOT))
    return o.reshape(b * seq_len, ML_V_TOT)


def _ffn_weights(w_up, conv_w, conv_b, w_down):
    f = w_down.shape[0]
    nblk = f // FFN_BLOCK

    def regroup(t):
        g = t[..., :f].reshape(t.shape[:-1] + (nblk, FFN_BLOCK))
        u = t[..., f:].reshape(t.shape[:-1] + (nblk, FFN_BLOCK))
        return jnp.moveaxis(jnp.concatenate([g, u], -1), -2, 0)

    return (regroup(w_up).astype(BF16), regroup(conv_w), regroup(conv_b[None]),
            w_down.reshape(nblk, FFN_BLOCK, -1).astype(BF16))


def kernel(x, meta_tokens, norm_w, gdn_w_in, gdn_conv_w, gdn_a_log, gdn_dt_bias, gdn_norm_w, gdn_w_out,
           ml_w_in, ml_gate_b, ml_norm_w, ml_w_out, ffn_w_up, ffn_conv_w, ffn_conv_b, ffn_w_down):
    b, s, d = x.shape
    depth = norm_w.shape[0]
    pad = jnp.zeros((b, N_PAD, d), x.dtype)
    meta = jnp.broadcast_to(meta_tokens[None].astype(x.dtype), (b, N_META, d))
    hs = jnp.concatenate([pad, meta, x], 1)
    seq_len = hs.shape[1]
    nc = seq_len // CHUNK
    nchunk = 3 if nc % 3 == 0 else 1
    for i in range(depth):
        j = i // 2
        nw = norm_w[i].reshape(4, 1, d)
        hs2d = hs.reshape(b * seq_len, d)
        if i % 2 == 0:
            mix = _gated_deltanet(hs2d, b, seq_len, nw[0], gdn_w_in[j], gdn_conv_w[j], gdn_a_log[j],
                                  gdn_dt_bias[j], gdn_norm_w[j], nchunk)
            w_out = gdn_w_out[j]
        else:
            mix = _mlstm(hs2d, b, seq_len, nw[0], ml_w_in[j], ml_gate_b[j], ml_norm_w[j], nchunk)
            w_out = ml_w_out[j]
        hs = _outproj(mix, w_out.astype(BF16), nw[1], hs2d, seq_len).reshape(b, seq_len, d)
        hs = _ffn(hs, nw[2], *_ffn_weights(ffn_w_up[i], ffn_conv_w[i], ffn_conv_b[i], ffn_w_down[i]), nw[3])
    return hs[:, N_PAD + N_META:]
```

```python
import functools

import jax
import jax.numpy as jnp
from jax import lax
from jax.experimental import pallas as pl
from jax.experimental.pallas import tpu as pltpu

F32 = jnp.float32
BF16 = jnp.bfloat16

CHUNK = 64
N_META = 16
N_PAD = CHUNK - N_META
RMS_EPS = 1e-6
L2_EPS = 1e-6

GDN_QK_HEADS = 8
GDN_V_HEADS = 16
GDN_HEAD_DIM = 128
GDN_CONV = 4
GDN_KEY_DIM = GDN_QK_HEADS * GDN_HEAD_DIM
GDN_VAL_DIM = GDN_V_HEADS * GDN_HEAD_DIM
GDN_QKV_DIM = 2 * GDN_KEY_DIM + GDN_VAL_DIM
GDN_MAIN = GDN_QKV_DIM + GDN_VAL_DIM

ML_HEADS = 4
ML_QK_DIM = 128
ML_V_DIM = 256
ML_GATE_CAP = 15.0
ML_QK_TOT = ML_HEADS * ML_QK_DIM
ML_V_TOT = ML_HEADS * ML_V_DIM
ML_MAIN = 2 * ML_QK_TOT + 2 * ML_V_TOT

FFN_CONV = 3
FFN_BLOCK = 256

LANES = 128
SUBLANES = 8
SOLVE_UNITS = 128
VMEM_LIMIT = 56 * 1024 * 1024


def _largest_tile(n, cap):
    d = (min(cap, n) // SUBLANES) * SUBLANES
    while d > SUBLANES and n % d:
        d -= SUBLANES
    assert d >= SUBLANES and n % d == 0, (n, cap)
    return d


def _params(*sem):
    return pltpu.CompilerParams(dimension_semantics=sem, vmem_limit_bytes=VMEM_LIMIT)


def _rms(x, w):
    return x * lax.rsqrt(jnp.mean(x * x, -1, keepdims=True) + RMS_EPS) * w


def _silu(x):
    return x * jax.nn.sigmoid(x)


def _dot(a, b):
    return jnp.dot(a, b, preferred_element_type=F32)


def _dot_nt(a, b):
    return lax.dot_general(a, b, (((1,), (1,)), ((), ())), preferred_element_type=F32)


def _dot_tn(a, b):
    return lax.dot_general(a, b, (((0,), (0,)), ((), ())), preferred_element_type=F32)


def _dot_f32(a, b):
    return jnp.dot(a, b, preferred_element_type=F32, precision=lax.Precision.HIGHEST)


def _tri_incl():
    r = lax.broadcasted_iota(jnp.int32, (CHUNK, CHUNK), 0)
    c = lax.broadcasted_iota(jnp.int32, (CHUNK, CHUNK), 1)
    return r, c


def _inproj_kernel(x_ref, nw_ref, wm_ref, wg_ref, om_ref, og_ref, a_scr):
    @pl.when(pl.program_id(1) == 0)
    def _():
        a = _rms(x_ref[...], nw_ref[...]).astype(BF16)
        a_scr[...] = a
        og_ref[...] = _dot(a, wg_ref[...])

    om_ref[...] = _dot(a_scr[...], wm_ref[...])


def _inproj(x2d, nw, w_main, w_gate, tn):
    m, d = x2d.shape
    n = w_main.shape[1]
    tm = _largest_tile(m, 1400)
    return pl.pallas_call(
        _inproj_kernel,
        grid=(m // tm, n // tn),
        in_specs=[
            pl.BlockSpec((tm, d), lambda i, j: (i, 0)),
            pl.BlockSpec((1, d), lambda i, j: (0, 0)),
            pl.BlockSpec((d, tn), lambda i, j: (0, j)),
            pl.BlockSpec((d, LANES), lambda i, j: (0, 0)),
        ],
        out_specs=[
            pl.BlockSpec((tm, tn), lambda i, j: (i, j)),
            pl.BlockSpec((tm, LANES), lambda i, j: (i, 0)),
        ],
        out_shape=[jax.ShapeDtypeStruct((m, n), F32), jax.ShapeDtypeStruct((m, LANES), F32)],
        scratch_shapes=[pltpu.VMEM((tm, d), BF16)],
        compiler_params=_params("parallel", "arbitrary"),
        name="inproj",
    )(x2d, nw, w_main, w_gate)


def _outproj_kernel(o_ref, w_ref, nw_ref, hs_ref, out_ref, *, tm, seq_len):
    y = _rms(_dot(o_ref[...], w_ref[...]), nw_ref[...])
    t = (pl.program_id(0) * tm) % seq_len + lax.broadcasted_iota(jnp.int32, (tm, 1), 0)
    out_ref[...] = hs_ref[...] + jnp.where(t >= N_PAD, y, 0.0)


def _outproj(o2d, w_out, nw, hs2d, seq_len):
    m, kin = o2d.shape
    d = hs2d.shape[1]
    tm = _largest_tile(seq_len, 1400)
    return pl.pallas_call(
        functools.partial(_outproj_kernel, tm=tm, seq_len=seq_len),
        grid=(m // tm,),
        in_specs=[
            pl.BlockSpec((tm, kin), lambda i: (i, 0)),
            pl.BlockSpec((kin, d), lambda i: (0, 0)),
            pl.BlockSpec((1, d), lambda i: (0, 0)),
            pl.BlockSpec((tm, d), lambda i: (i, 0)),
        ],
        out_specs=pl.BlockSpec((tm, d), lambda i: (i, 0)),
        out_shape=jax.ShapeDtypeStruct((m, d), F32),
        compiler_params=_params("parallel"),
        name="outproj",
    )(o2d, w_out, nw, hs2d)


def _ffn_kernel(h_ref, halo_ref, nw2_ref, wup_ref, cw_ref, cb_ref, wdn_ref, nw3_ref, out_ref, *, tm):
    i = pl.program_id(1)
    h = h_ref[0]
    halo = jnp.where(i == 0, 0.0, halo_ref[0])
    a = _rms(jnp.concatenate([halo, h], 0), nw2_ref[...]).astype(BF16)
    half = tm // 2
    a_lo, a_hi = a[:half + SUBLANES], a[half:]
    fdim = wdn_ref.shape[0]
    cols = [(slice(j * FFN_BLOCK, (j + 1) * FFN_BLOCK), slice(fdim + j * FFN_BLOCK, fdim + (j + 1) * FFN_BLOCK))
            for j in range(fdim // FFN_BLOCK)]

    def up(a_x, j):
        return [_dot(a_x, wup_ref[:, sl]) for sl in cols[j]]

    def conv(u, sl):
        cw = cw_ref[:, sl]
        return (cw[2:3] * u[SUBLANES:]
                + cw[1:2] * pltpu.roll(u, 1, 0)[SUBLANES:]
                + cw[0:1] * pltpu.roll(u, 2, 0)[SUBLANES:]
                + cb_ref[:, sl])

    def down(u, j):
        act = (_silu(conv(u[0], cols[j][0])) * conv(u[1], cols[j][1])).astype(BF16)
        return _dot(act, wdn_ref[cols[j][0], :])

    acc_lo = acc_hi = 0.0
    u_lo = up(a_lo, 0)
    for j in range(len(cols)):
        u_hi = up(a_hi, j)
        acc_lo = acc_lo + down(u_lo, j)
        if j + 1 < len(cols):
            u_lo = up(a_lo, j + 1)
        acc_hi = acc_hi + down(u_hi, j)
    f = jnp.concatenate([acc_lo, acc_hi], 0)
    t = i * tm + lax.broadcasted_iota(jnp.int32, (tm, 1), 0)
    out_ref[0] = h + jnp.where(t >= N_PAD, _rms(f, nw3_ref[...]), 0.0)


def _ffn(hs, nw2, w_up, conv_w, conv_b, w_down, nw3):
    b, seq_len, d = hs.shape
    tm = _largest_tile(seq_len, 700)
    hb = tm // SUBLANES
    assert tm % (2 * SUBLANES) == 0 and w_down.shape[0] % FFN_BLOCK == 0
    const = lambda bi, i: (0, 0)
    return pl.pallas_call(
        functools.partial(_ffn_kernel, tm=tm),
        grid=(b, seq_len // tm),
        in_specs=[
            pl.BlockSpec((1, tm, d), lambda bi, i: (bi, i, 0)),
            pl.BlockSpec((1, SUBLANES, d), lambda bi, i: (bi, jnp.maximum(i * hb - 1, 0), 0)),
            pl.BlockSpec((1, d), const),
            pl.BlockSpec(w_up.shape, const),
            pl.BlockSpec(conv_w.shape, const),
            pl.BlockSpec(conv_b.shape, const),
            pl.BlockSpec(w_down.shape, const),
            pl.BlockSpec((1, d), const),
        ],
        out_specs=pl.BlockSpec((1, tm, d), lambda bi, i: (bi, i, 0)),
        out_shape=jax.ShapeDtypeStruct(hs.shape, F32),
        compiler_params=_params("parallel", "parallel"),
        name="convffn",
    )(hs, hs, nw2, w_up, conv_w, conv_b, w_down, nw3)


def _gdn_gates_col(graw, prm_ref):
    beta = jax.nn.sigmoid(graw)
    g = -jnp.exp(prm_ref[0:1, :]) * jax.nn.softplus(graw + prm_ref[1:2, :])
    return beta, g


def _gdn_prep_kernel(qk_ref, v_ref, qkh_ref, vh_ref, cwqk_ref, cwv_ref, gc_ref, gr_ref, prow_ref, pcol_ref,
                     qn_ref, kn_ref, vv_ref, gco_ref, gro_ref, a_ref, *, nchunk):
    i = pl.program_id(1)
    rows = nchunk * CHUNK
    t = i * rows + lax.broadcasted_iota(jnp.int32, (rows, 1), 0)
    valid = (t >= N_PAD).astype(F32)
    first = i == 0

    def conv_silu(x_ref, halo_ref, cw_ref, col):
        sl = slice(col * GDN_HEAD_DIM, (col + 1) * GDN_HEAD_DIM)
        halo = jnp.where(first, 0.0, halo_ref[0, :, sl])
        x = jnp.concatenate([halo, x_ref[0, :, sl]], 0)
        cw = cw_ref[:, sl]
        y = (cw[3:4] * x[SUBLANES:]
             + cw[2:3] * pltpu.roll(x, 1, 0)[SUBLANES:]
             + cw[1:2] * pltpu.roll(x, 2, 0)[SUBLANES:]
             + cw[0:1] * pltpu.roll(x, 3, 0)[SUBLANES:])
        return _silu(y)

    def l2n(x):
        return x * lax.rsqrt(jnp.sum(x * x, -1, keepdims=True) + L2_EPS)

    for hq in range(GDN_QK_HEADS):
        sl = slice(hq * GDN_HEAD_DIM, (hq + 1) * GDN_HEAD_DIM)
        qn_ref[0, :, sl] = l2n(conv_silu(qk_ref, qkh_ref, cwqk_ref, hq)) * (GDN_HEAD_DIM ** -0.5)
        kn_ref[0, :, sl] = l2n(conv_silu(qk_ref, qkh_ref, cwqk_ref, GDN_QK_HEADS + hq)) * valid
    for h in range(GDN_V_HEADS):
        sl = slice(h * GDN_HEAD_DIM, (h + 1) * GDN_HEAD_DIM)
        vv_ref[0, :, sl] = conv_silu(v_ref, vh_ref, cwv_ref, h) * valid

    r, c = _tri_incl()
    tri = (r >= c).astype(F32)
    tri_t = (r <= c).astype(F32)
    incl = r >= c
    strict = r > c
    rep = GDN_V_HEADS // GDN_QK_HEADS
    for ci in range(nchunk):
        rs = slice(ci * CHUNK, (ci + 1) * CHUNK)
        beta, g = _gdn_gates_col(gc_ref[0, rs, :], prow_ref)
        gcum = _dot_f32(tri, g)
        lane = lax.broadcasted_iota(jnp.int32, (CHUNK, LANES), 1)
        gco_ref[0, rs, :] = jnp.where(lane < GDN_V_HEADS, beta, gcum)
        graw_r = gr_ref[0, ci]
        g_r = -jnp.exp(pcol_ref[:, 0:1]) * jax.nn.softplus(graw_r + pcol_ref[:, 1:2])
        gcum_r = _dot_f32(g_r, tri_t)
        gro_ref[0, ci] = gcum_r
        for hq in range(GDN_QK_HEADS):
            kc = kn_ref[0, rs, hq * GDN_HEAD_DIM:(hq + 1) * GDN_HEAD_DIM].astype(BF16)
            kk = _dot_nt(kc, kc)
            pair = []
            for e in range(rep):
                h = hq * rep + e
                gi = gcum[:, GDN_V_HEADS + h:GDN_V_HEADS + h + 1]
                gj = gcum_r[GDN_V_HEADS + h:GDN_V_HEADS + h + 1, :]
                dec = jnp.exp(jnp.where(incl, gi - gj, -jnp.inf))
                pair.append(jnp.where(strict, kk * beta[:, h:h + 1] * dec, 0.0))
            a_ref[0, ci, hq] = jnp.concatenate(pair, 1)


def _gdn_prep(proj, gates, gates_r, conv_w, prm_row, prm_col, nchunk):
    b, seq_len, _ = proj.shape
    nc = seq_len // CHUNK
    rows = nchunk * CHUNK
    hb = rows // SUBLANES
    half = GDN_QKV_DIM // 2
    halo_map0 = lambda bi, i: (bi, jnp.maximum(i * hb - 1, 0), 0)
    halo_map1 = lambda bi, i: (bi, jnp.maximum(i * hb - 1, 0), 1)
    return pl.pallas_call(
        functools.partial(_gdn_prep_kernel, nchunk=nchunk),
        grid=(b, nc // nchunk),
        in_specs=[
            pl.BlockSpec((1, rows, half), lambda bi, i: (bi, i, 0)),
            pl.BlockSpec((1, rows, half), lambda bi, i: (bi, i, 1)),
            pl.BlockSpec((1, SUBLANES, half), halo_map0),
            pl.BlockSpec((1, SUBLANES, half), halo_map1),
            pl.BlockSpec((GDN_CONV, half), lambda bi, i: (0, 0)),
            pl.BlockSpec((GDN_CONV, half), lambda bi, i: (0, 1)),
            pl.BlockSpec((1, rows, LANES), lambda bi, i: (bi, i, 0)),
            pl.BlockSpec((1, nchunk, 2 * GDN_V_HEADS, CHUNK), lambda bi, i: (bi, i, 0, 0)),
            pl.BlockSpec((SUBLANES, LANES), lambda bi, i: (0, 0)),
            pl.BlockSpec((2 * GDN_V_HEADS, LANES), lambda bi, i: (0, 0)),
        ],
        out_specs=[
            pl.BlockSpec((1, rows, GDN_KEY_DIM), lambda bi, i: (bi, i, 0)),
            pl.BlockSpec((1, rows, GDN_KEY_DIM), lambda bi, i: (bi, i, 0)),
            pl.BlockSpec((1, rows, GDN_VAL_DIM), lambda bi, i: (bi, i, 0)),
            pl.BlockSpec((1, rows, LANES), lambda bi, i: (bi, i, 0)),
            pl.BlockSpec((1, nchunk, 2 * GDN_V_HEADS, CHUNK), lambda bi, i: (bi, i, 0, 0)),
            pl.BlockSpec((1, nchunk, GDN_QK_HEADS, CHUNK, LANES), lambda bi, i: (bi, i, 0, 0, 0)),
        ],
        out_shape=[
            jax.ShapeDtypeStruct((b, seq_len, GDN_KEY_DIM), F32),
            jax.ShapeDtypeStruct((b, seq_len, GDN_KEY_DIM), F32),
            jax.ShapeDtypeStruct((b, seq_len, GDN_VAL_DIM), F32),
            jax.ShapeDtypeStruct((b, seq_len, LANES), F32),
            jax.ShapeDtypeStruct((b, nc, 2 * GDN_V_HEADS, CHUNK), F32),
            jax.ShapeDtypeStruct((b, nc, GDN_QK_HEADS, CHUNK, LANES), F32),
        ],
        compiler_params=_params("parallel", "parallel"),
        name="gdn_prep",
    )(proj, proj, proj, proj, conv_w, conv_w, gates, gates_r, prm_row, prm_col)


def _tri_inv_kernel(a_ref, x_ref, at_scr, xt_scr):
    for i in range(CHUNK):
        at_scr[i] = a_ref[pl.ds(i, SOLVE_UNITS, stride=CHUNK), :].T

    xt_scr[...] = jnp.zeros_like(xt_scr)
    jrow = lax.broadcasted_iota(jnp.int32, (SUBLANES, SOLVE_UNITS), 0)
    sub = SUBLANES
    heads = LANES // CHUNK
    for ib in range(CHUNK // sub):
        def row_body(ii, carry, ib=ib):
            i = ib * sub + ii
            for e in range(heads):
                base = e * CHUNK
                acc = [(jrow + jb * sub == i).astype(F32) for jb in range(ib + 1)]
                for kb in range(ib + 1):
                    for kk in range(sub):
                        k = base + kb * sub + kk
                        a = at_scr[i, k:k + 1, :]
                        for jb in range(kb + 1):
                            acc[jb] = acc[jb] - a * xt_scr[kb * sub + kk, base + jb * sub:base + (jb + 1) * sub, :]
                for jb in range(ib + 1):
                    xt_scr[i, base + jb * sub:base + (jb + 1) * sub, :] = acc[jb]
            return carry

        lax.fori_loop(0, sub, row_body, 0)

    for i in range(CHUNK):
        x_ref[pl.ds(i, SOLVE_UNITS, stride=CHUNK), :] = xt_scr[i].T


def _tri_inv(a_rows):
    blk = (SOLVE_UNITS * CHUNK, LANES)
    return pl.pallas_call(
        _tri_inv_kernel,
        grid=(pl.cdiv(a_rows.shape[0], blk[0]),),
        in_specs=[pl.BlockSpec(blk, lambda i: (i, 0))],
        out_specs=pl.BlockSpec(blk, lambda i: (i, 0)),
        out_shape=jax.ShapeDtypeStruct(a_rows.shape, F32),
        scratch_shapes=[pltpu.VMEM((CHUNK, LANES, SOLVE_UNITS), F32),
                        pltpu.VMEM((CHUNK, LANES, SOLVE_UNITS), F32)],
        compiler_params=_params("parallel"),
        name="tri_inv",
    )(a_rows)


def _gdn_scan_kernel(qn_ref, kn_ref, vv_ref, z_ref, gc_ref, gr_ref, ti_ref, nw_ref, o_ref, s_scr, *, nchunk):
    @pl.when(pl.program_id(1) == 0)
    def _():
        s_scr[...] = jnp.zeros_like(s_scr)

    r, c = _tri_incl()
    incl = r >= c
    rep = GDN_V_HEADS // GDN_QK_HEADS
    dh = GDN_HEAD_DIM
    nb = qn_ref.shape[0]
    chains = [(bi, h) for bi in range(nb) for h in range(GDN_V_HEADS)]

    def chunk_body(ci, carry):
        rs = pl.ds(pl.multiple_of(ci * CHUNK, CHUNK), CHUNK)
        pre = {}
        for bi in range(nb):
            gc = gc_ref[bi, rs, :]
            gr = gr_ref[bi, ci]
            for hq in range(GDN_QK_HEADS):
                q = qn_ref[bi, rs, hq * dh:(hq + 1) * dh]
                k = kn_ref[bi, rs, hq * dh:(hq + 1) * dh]
                qk = _dot_nt(q.astype(BF16), k.astype(BF16))
                tinv = ti_ref[bi, ci, hq].astype(BF16)
                for e in range(rep):
                    h = hq * rep + e
                    beta = gc[:, h:h + 1]
                    gi = gc[:, GDN_V_HEADS + h:GDN_V_HEADS + h + 1]
                    gj = gr[GDN_V_HEADS + h:GDN_V_HEADS + h + 1, :]
                    glast = gi[CHUNK - 1:CHUNK, :]
                    egi = jnp.exp(gi)
                    attn = (qk * jnp.exp(jnp.where(incl, gi - gj, -jnp.inf))).astype(BF16)
                    v = vv_ref[bi, rs, h * dh:(h + 1) * dh]
                    rhs = jnp.concatenate([v * beta, k * (beta * egi)], 1).astype(BF16)
                    sol = _dot(tinv[:, e * CHUNK:(e + 1) * CHUNK], rhs)
                    u, w = sol[:, :dh], sol[:, dh:]
                    wq = jnp.concatenate([w, q * egi], 0).astype(BF16)
                    kt = (k * jnp.exp(glast - gi)).astype(BF16)
                    pre[bi, h] = (u, wq, attn, kt, jnp.exp(glast))
        mid = {}
        for bi, h in chains:
            u, wq, _, _, _ = pre[bi, h]
            ws_qs = _dot(wq, s_scr[bi * GDN_V_HEADS + h].astype(BF16))
            mid[bi, h] = (ws_qs[CHUNK:], (u - ws_qs[:CHUNK]).astype(BF16))
        for bi, h in chains:
            _, _, attn, kt, gt = pre[bi, h]
            qs, vb = mid[bi, h]
            hs = slice(h * dh, (h + 1) * dh)
            o = qs + _dot(attn, vb)
            si = bi * GDN_V_HEADS + h
            s_scr[si] = s_scr[si] * gt + _dot_tn(kt, vb)
            o_ref[bi, rs, hs] = (_rms(o, nw_ref[...]) * _silu(z_ref[bi, rs, hs])).astype(BF16)
        return carry

    lax.fori_loop(0, nchunk, chunk_body, 0)


def _gdn_scan(qn, kn, vv, proj, gco, gro, tinv, norm_w, nchunk):
    b, seq_len, _ = qn.shape
    nc = seq_len // CHUNK
    rows = nchunk * CHUNK
    zblk = GDN_QKV_DIM // GDN_VAL_DIM
    return pl.pallas_call(
        functools.partial(_gdn_scan_kernel, nchunk=nchunk),
        grid=(b, nc // nchunk),
        in_specs=[
            pl.BlockSpec((1, rows, GDN_KEY_DIM), lambda bi, i: (bi, i, 0)),
            pl.BlockSpec((1, rows, GDN_KEY_DIM), lambda bi, i: (bi, i, 0)),
            pl.BlockSpec((1, rows, GDN_VAL_DIM), lambda bi, i: (bi, i, 0)),
            pl.BlockSpec((1, rows, GDN_VAL_DIM), lambda bi, i: (bi, i, zblk)),
            pl.BlockSpec((1, rows, LANES), lambda bi, i: (bi, i, 0)),
            pl.BlockSpec((1, nchunk, 2 * GDN_V_HEADS, CHUNK), lambda bi, i: (bi, i, 0, 0)),
            pl.BlockSpec((1, nchunk, GDN_QK_HEADS, CHUNK, LANES), lambda bi, i: (bi, i, 0, 0, 0)),
            pl.BlockSpec((1, GDN_HEAD_DIM), lambda bi, i: (0, 0)),
        ],
        out_specs=pl.BlockSpec((1, rows, GDN_VAL_DIM), lambda bi, i: (bi, i, 0)),
        out_shape=jax.ShapeDtypeStruct((b, seq_len, GDN_VAL_DIM), BF16),
        scratch_shapes=[pltpu.VMEM((GDN_V_HEADS, GDN_HEAD_DIM, GDN_HEAD_DIM), F32)],
        compiler_params=_params("parallel", "arbitrary"),
        name="gdn_scan",
    )(qn, kn, vv, proj, gco, gro, tinv, norm_w)


def _gated_deltanet(a2d, b, seq_len, nw_in, w_in, conv_w, a_log, dt_bias, norm_w, nchunk):
    nc = seq_len // CHUNK
    ng = 2 * GDN_V_HEADS
    w_main = w_in[:, :GDN_MAIN].astype(BF16)
    w_gate = jnp.pad(w_in[:, GDN_MAIN:], ((0, 0), (0, LANES - ng))).astype(BF16)
    proj, gates = _inproj(a2d, nw_in, w_main, w_gate, 1024)
    proj = proj.reshape(b, seq_len, GDN_MAIN)
    gates = gates.reshape(b, seq_len, LANES)
    gates_r = jnp.swapaxes(gates[..., :ng].reshape(b, nc, CHUNK, ng), 2, 3)
    pad16 = jnp.zeros((GDN_V_HEADS,), F32)
    prm_row = jnp.zeros((SUBLANES, LANES), F32)
    prm_row = prm_row.at[0, :ng].set(jnp.concatenate([pad16, a_log]))
    prm_row = prm_row.at[1, :ng].set(jnp.concatenate([pad16, dt_bias]))
    prm_col = jnp.zeros((ng, LANES), F32)
    prm_col = prm_col.at[:, 0].set(jnp.concatenate([pad16, a_log]))
    prm_col = prm_col.at[:, 1].set(jnp.concatenate([pad16, dt_bias]))
    qn, kn, vv, gco, gro, a_mat = _gdn_prep(proj, gates, gates_r, conv_w, prm_row, prm_col, nchunk)
    assert (GDN_V_HEADS // GDN_QK_HEADS) * CHUNK == LANES
    tinv = _tri_inv(a_mat.reshape(b * nc * GDN_QK_HEADS * CHUNK, LANES)).reshape(a_mat.shape)
    o = _gdn_scan(qn, kn, vv, proj, gco, gro, tinv, norm_w.reshape(1, GDN_HEAD_DIM), nchunk)
    return o.reshape(b * seq_len, GDN_VAL_DIM)


def _mlstm_kernel(q_ref, k_ref, v_ref, og_ref, gc_ref, gr_ref, brow_ref, bcol_ref, nw_ref, o_ref,
                  c_scr, n_scr, m_scr, *, nchunk):
    i = pl.program_id(1)

    @pl.when(i == 0)
    def _():
        c_scr[...] = jnp.zeros_like(c_scr)
        n_scr[...] = jnp.zeros_like(n_scr)
        m_scr[...] = jnp.zeros_like(m_scr)

    r, c = _tri_incl()
    incl = r >= c
    tri = incl.astype(F32)
    tri_t = (r <= c).astype(F32)
    dk, dv = ML_QK_DIM, ML_V_DIM
    nb = q_ref.shape[0]
    chains = [(bi, h) for bi in range(nb) for h in range(ML_HEADS)]

    def chunk_body(ci, carry):
        rs = pl.ds(pl.multiple_of(ci * CHUNK, CHUNK), CHUNK)
        t = (i * nchunk + ci) * CHUNK + lax.broadcasted_iota(jnp.int32, (CHUNK, 1), 0)
        valid = (t >= N_PAD).astype(F32)
        pre = {}
        for bi in range(nb):
            gcap = ML_GATE_CAP * jnp.tanh((gc_ref[bi, rs, :] + brow_ref[0:1, :]) / ML_GATE_CAP)
            bcum = _dot_f32(tri, jax.nn.log_sigmoid(gcap))
            gcap_r = ML_GATE_CAP * jnp.tanh((gr_ref[bi, ci] + bcol_ref[:, 0:1]) / ML_GATE_CAP)
            bcum_r = _dot_f32(jax.nn.log_sigmoid(gcap_r), tri_t)
            for h in range(ML_HEADS):
                b_i = bcum[:, ML_HEADS + h:ML_HEADS + h + 1]
                b_j = bcum_r[ML_HEADS + h:ML_HEADS + h + 1, :]
                li_i = gcap[:, h:h + 1]
                li_j = gcap_r[h:h + 1, :]
                bl = b_i[CHUNK - 1:CHUNK, :]
                a_i = bl - b_i + li_i
                ms = m_scr[bi * ML_HEADS + h]
                q = q_ref[bi, rs, h * dk:(h + 1) * dk] * (dk ** -0.5)
                k = k_ref[bi, rs, h * dk:(h + 1) * dk] * valid
                v = v_ref[bi, rs, h * dv:(h + 1) * dv] * valid
                qb, kb, vb = q.astype(BF16), k.astype(BF16), v.astype(BF16)
                qk = _dot_nt(qb, kb)
                m_new = jnp.maximum(bl + ms, jnp.max(a_i, 0, keepdims=True))
                dec = jnp.exp(bl + ms - m_new)
                kw = k * jnp.exp(a_i - m_new)
                kv = _dot_tn(kw.astype(BF16), vb)
                d_mat = jnp.where(incl, b_i - b_j + li_j, -jnp.inf)
                inter = b_i + ms
                m_t = jnp.maximum(jnp.max(d_mat, -1, keepdims=True), inter)
                pre[bi, h] = (q, qb, vb, qk, kv, kw, d_mat, inter, m_t, m_new, dec)
        mid = {}
        for ch in chains:
            q, qb, vb, qk, kv, kw, d_mat, inter, m_t, m_new, dec = pre[ch]
            w_d = jnp.exp(d_mat - m_t) * qk
            mid[ch] = (_dot(w_d.astype(BF16), vb), jnp.sum(w_d, -1, keepdims=True))
        for bi, h in chains:
            q, qb, vb, qk, kv, kw, d_mat, inter, m_t, m_new, dec = pre[bi, h]
            wv, wsum = mid[bi, h]
            si = bi * ML_HEADS + h
            cs = c_scr[si]
            ns = n_scr[si]
            sc = jnp.exp(inter - m_t)
            num = sc * _dot(qb, cs.astype(BF16)) + wv
            den = sc * jnp.sum(q * ns, -1, keepdims=True) + wsum
            hh = num / jnp.maximum(jnp.abs(den), jnp.exp(-m_t))
            c_scr[si] = dec * cs + kv
            n_scr[si] = dec * ns + jnp.sum(kw, 0, keepdims=True)
            m_scr[si] = m_new
            vs = slice(h * dv, (h + 1) * dv)
            o_ref[bi, rs, vs] = (_rms(hh, nw_ref[:, vs]) * jax.nn.sigmoid(og_ref[bi, rs, vs])).astype(BF16)
        return carry

    lax.fori_loop(0, nchunk, chunk_body, 0)


def _mlstm(a2d, b, seq_len, nw_in, w_in, gate_b, norm_w, nchunk):
    nc = seq_len // CHUNK
    ng = 2 * ML_HEADS
    w_main = w_in[:, :ML_MAIN].astype(BF16)
    w_gate = jnp.pad(w_in[:, ML_MAIN:], ((0, 0), (0, LANES - ng))).astype(BF16)
    proj, gates = _inproj(a2d, nw_in, w_main, w_gate, 1024)
    proj = proj.reshape(b, seq_len, ML_MAIN)
    gates = gates.reshape(b, seq_len, LANES)
    gates_r = jnp.swapaxes(gates[..., :ng].reshape(b, nc, CHUNK, ng), 2, 3)
    b_row = jnp.zeros((SUBLANES, LANES), F32).at[0, :ng].set(gate_b)
    b_col = jnp.zeros((ng, LANES), F32).at[:, 0].set(gate_b)
    rows = nchunk * CHUNK
    qkb = ML_QK_TOT
    o = pl.pallas_call(
        functools.partial(_mlstm_kernel, nchunk=nchunk),
        grid=(b, nc // nchunk),
        in_specs=[
            pl.BlockSpec((1, rows, qkb), lambda bi, i: (bi, i, 0)),
            pl.BlockSpec((1, rows, qkb), lambda bi, i: (bi, i, 1)),
            pl.BlockSpec((1, rows, ML_V_TOT), lambda bi, i: (bi, i, 1)),
            pl.BlockSpec((1, rows, ML_V_TOT), lambda bi, i: (bi, i, 2)),
            pl.BlockSpec((1, rows, LANES), lambda bi, i: (bi, i, 0)),
            pl.BlockSpec((1, nchunk, ng, CHUNK), lambda bi, i: (bi, i, 0, 0)),
            pl.BlockSpec((SUBLANES, LANES), lambda bi, i: (0, 0)),
            pl.BlockSpec((ng, LANES), lambda bi, i: (0, 0)),
            pl.BlockSpec((1, ML_V_TOT), lambda bi, i: (0, 0)),
        ],
        out_specs=pl.BlockSpec((1, rows, ML_V_TOT), lambda bi, i: (bi, i, 0)),
        out_shape=jax.ShapeDtypeStruct((b, seq_len, ML_V_TOT), BF16),
        scratch_shapes=[pltpu.VMEM((ML_HEADS, ML_QK_DIM, ML_V_DIM), F32),
                        pltpu.VMEM((ML_HEADS, 1, ML_QK_DIM), F32),
                        pltpu.VMEM((ML_HEADS, 1, 1), F32)],
        compiler_params=_params("parallel", "arbitrary"),
        name="mlstm",
    )(proj, proj, proj, proj, gates, gates_r, b_row, b_col, norm_w.reshape(1, ML_V_TOT))
    return o.reshape(b * seq_len, ML_V_TOT)


def kernel(x, meta_tokens, norm_w, gdn_w_in, gdn_conv_w, gdn_a_log, gdn_dt_bias, gdn_norm_w, gdn_w_out,
           ml_w_in, ml_gate_b, ml_norm_w, ml_w_out, ffn_w_up, ffn_conv_w, ffn_conv_b, ffn_w_down):
    b, s, d = x.shape
    depth = norm_w.shape[0]
    pad = jnp.zeros((b, N_PAD, d), x.dtype)
    meta = jnp.broadcast_to(meta_tokens[None].astype(x.dtype), (b, N_META, d))
    hs = jnp.concatenate([pad, meta, x], 1)
    seq_len = hs.shape[1]
    nc = seq_len // CHUNK
    nchunk = 3 if nc % 3 == 0 else 1
    for i in range(depth):
        j = i // 2
        nw = norm_w[i].reshape(4, 1, d)
        hs2d = hs.reshape(b * seq_len, d)
        if i % 2 == 0:
            mix = _gated_deltanet(hs2d, b, seq_len, nw[0], gdn_w_in[j], gdn_conv_w[j], gdn_a_log[j],
                                  gdn_dt_bias[j], gdn_norm_w[j], nchunk)
            w_out = gdn_w_out[j]
        else:
            mix = _mlstm(hs2d, b, seq_len, nw[0], ml_w_in[j], ml_gate_b[j], ml_norm_w[j], nchunk)
            w_out = ml_w_out[j]
        hs = _outproj(mix, w_out.astype(BF16), nw[1], hs2d, seq_len).reshape(b, seq_len, d)
        hs = _ffn(hs, nw[2], ffn_w_up[i].astype(BF16), ffn_conv_w[i], ffn_conv_b[i][None],
                  ffn_w_down[i].astype(BF16), nw[3])
    return hs[:, N_PAD + N_META:]
```

```python
import functools

import jax
import jax.numpy as jnp
from jax import lax
from jax.experimental import pallas as pl
from jax.experimental.pallas import tpu as pltpu

F32 = jnp.float32
BF16 = jnp.bfloat16

CHUNK = 64
N_META = 16
N_PAD = CHUNK - N_META
RMS_EPS = 1e-6
L2_EPS = 1e-6

GDN_QK_HEADS = 8
GDN_V_HEADS = 16
GDN_HEAD_DIM = 128
GDN_CONV = 4
GDN_KEY_DIM = GDN_QK_HEADS * GDN_HEAD_DIM
GDN_VAL_DIM = GDN_V_HEADS * GDN_HEAD_DIM
GDN_QKV_DIM = 2 * GDN_KEY_DIM + GDN_VAL_DIM
GDN_MAIN = GDN_QKV_DIM + GDN_VAL_DIM

ML_HEADS = 4
ML_QK_DIM = 128
ML_V_DIM = 256
ML_GATE_CAP = 15.0
ML_QK_TOT = ML_HEADS * ML_QK_DIM
ML_V_TOT = ML_HEADS * ML_V_DIM
ML_MAIN = 2 * ML_QK_TOT + 2 * ML_V_TOT

FFN_CONV = 3
FFN_BLOCK = 256

LANES = 128
SUBLANES = 8
SOLVE_UNITS = 128
VMEM_LIMIT = 56 * 1024 * 1024


def _largest_tile(n, cap):
    d = (min(cap, n) // SUBLANES) * SUBLANES
    while d > SUBLANES and n % d:
        d -= SUBLANES
    assert d >= SUBLANES and n % d == 0, (n, cap)
    return d


def _params(*sem):
    return pltpu.CompilerParams(dimension_semantics=sem, vmem_limit_bytes=VMEM_LIMIT)


def _rms(x, w):
    return x * lax.rsqrt(jnp.mean(x * x, -1, keepdims=True) + RMS_EPS) * w


def _silu(x):
    return x * jax.nn.sigmoid(x)


def _dot(a, b):
    return jnp.dot(a, b, preferred_element_type=F32)


def _dot_nt(a, b):
    return lax.dot_general(a, b, (((1,), (1,)), ((), ())), preferred_element_type=F32)


def _dot_tn(a, b):
    return lax.dot_general(a, b, (((0,), (0,)), ((), ())), preferred_element_type=F32)


def _dot_f32(a, b):
    return jnp.dot(a, b, preferred_element_type=F32, precision=lax.Precision.HIGHEST)


def _tri_incl():
    r = lax.broadcasted_iota(jnp.int32, (CHUNK, CHUNK), 0)
    c = lax.broadcasted_iota(jnp.int32, (CHUNK, CHUNK), 1)
    return r, c


def _inproj_kernel(x_ref, nw_ref, wm_ref, wg_ref, om_ref, og_ref, a_scr):
    @pl.when(pl.program_id(1) == 0)
    def _():
        a = _rms(x_ref[...], nw_ref[...]).astype(BF16)
        a_scr[...] = a
        og_ref[...] = _dot(a, wg_ref[...])

    om_ref[...] = _dot(a_scr[...], wm_ref[...])


def _inproj(x2d, nw, w_main, w_gate, tn):
    m, d = x2d.shape
    n = w_main.shape[1]
    tm = _largest_tile(m, 1400)
    return pl.pallas_call(
        _inproj_kernel,
        grid=(m // tm, n // tn),
        in_specs=[
            pl.BlockSpec((tm, d), lambda i, j: (i, 0)),
            pl.BlockSpec((1, d), lambda i, j: (0, 0)),
            pl.BlockSpec((d, tn), lambda i, j: (0, j)),
            pl.BlockSpec((d, LANES), lambda i, j: (0, 0)),
        ],
        out_specs=[
            pl.BlockSpec((tm, tn), lambda i, j: (i, j)),
            pl.BlockSpec((tm, LANES), lambda i, j: (i, 0)),
        ],
        out_shape=[jax.ShapeDtypeStruct((m, n), F32), jax.ShapeDtypeStruct((m, LANES), F32)],
        scratch_shapes=[pltpu.VMEM((tm, d), BF16)],
        compiler_params=_params("parallel", "arbitrary"),
        name="inproj",
    )(x2d, nw, w_main, w_gate)


def _outproj_kernel(o_ref, w_ref, nw_ref, hs_ref, out_ref, *, tm, seq_len):
    y = _rms(_dot(o_ref[...], w_ref[...]), nw_ref[...])
    t = (pl.program_id(0) * tm) % seq_len + lax.broadcasted_iota(jnp.int32, (tm, 1), 0)
    out_ref[...] = hs_ref[...] + jnp.where(t >= N_PAD, y, 0.0)


def _outproj(o2d, w_out, nw, hs2d, seq_len):
    m, kin = o2d.shape
    d = hs2d.shape[1]
    tm = _largest_tile(seq_len, 1400)
    return pl.pallas_call(
        functools.partial(_outproj_kernel, tm=tm, seq_len=seq_len),
        grid=(m // tm,),
        in_specs=[
            pl.BlockSpec((tm, kin), lambda i: (i, 0)),
            pl.BlockSpec((kin, d), lambda i: (0, 0)),
            pl.BlockSpec((1, d), lambda i: (0, 0)),
            pl.BlockSpec((tm, d), lambda i: (i, 0)),
        ],
        out_specs=pl.BlockSpec((tm, d), lambda i: (i, 0)),
        out_shape=jax.ShapeDtypeStruct((m, d), F32),
        compiler_params=_params("parallel"),
        name="outproj",
    )(o2d, w_out, nw, hs2d)


def _ffn_kernel(h_ref, halo_ref, nw2_ref, wup_ref, cw_ref, cb_ref, wdn_ref, nw3_ref, out_ref, *, tm):
    i = pl.program_id(1)
    h = h_ref[0]
    halo = jnp.where(i == 0, 0.0, halo_ref[0])
    a = _rms(jnp.concatenate([halo, h], 0), nw2_ref[...]).astype(BF16)
    half = tm // 2
    a_lo, a_hi = a[:half + SUBLANES], a[half:]
    fdim = wdn_ref.shape[0]
    cols = [(slice(j * FFN_BLOCK, (j + 1) * FFN_BLOCK), slice(fdim + j * FFN_BLOCK, fdim + (j + 1) * FFN_BLOCK))
            for j in range(fdim // FFN_BLOCK)]

    def up(a_x, j):
        return [_dot(a_x, wup_ref[:, sl]) for sl in cols[j]]

    def conv(u, sl):
        cw = cw_ref[:, sl]
        return (cw[2:3] * u[SUBLANES:]
                + cw[1:2] * pltpu.roll(u, 1, 0)[SUBLANES:]
                + cw[0:1] * pltpu.roll(u, 2, 0)[SUBLANES:]
                + cb_ref[:, sl])

    def down(u, j):
        act = (_silu(conv(u[0], cols[j][0])) * conv(u[1], cols[j][1])).astype(BF16)
        return _dot(act, wdn_ref[cols[j][0], :])

    acc_lo = acc_hi = 0.0
    u_lo = up(a_lo, 0)
    for j in range(len(cols)):
        u_hi = up(a_hi, j)
        acc_lo = acc_lo + down(u_lo, j)
        if j + 1 < len(cols):
            u_lo = up(a_lo, j + 1)
        acc_hi = acc_hi + down(u_hi, j)
    f = jnp.concatenate([acc_lo, acc_hi], 0)
    t = i * tm + lax.broadcasted_iota(jnp.int32, (tm, 1), 0)
    out_ref[0] = h + jnp.where(t >= N_PAD, _rms(f, nw3_ref[...]), 0.0)


def _ffn(hs, nw2, w_up, conv_w, conv_b, w_down, nw3):
    b, seq_len, d = hs.shape
    tm = _largest_tile(seq_len, 700)
    hb = tm // SUBLANES
    assert tm % (2 * SUBLANES) == 0 and w_down.shape[0] % FFN_BLOCK == 0
    const = lambda bi, i: (0, 0)
    return pl.pallas_call(
        functools.partial(_ffn_kernel, tm=tm),
        grid=(b, seq_len // tm),
        in_specs=[
            pl.BlockSpec((1, tm, d), lambda bi, i: (bi, i, 0)),
            pl.BlockSpec((1, SUBLANES, d), lambda bi, i: (bi, jnp.maximum(i * hb - 1, 0), 0)),
            pl.BlockSpec((1, d), const),
            pl.BlockSpec(w_up.shape, const),
            pl.BlockSpec(conv_w.shape, const),
            pl.BlockSpec(conv_b.shape, const),
            pl.BlockSpec(w_down.shape, const),
            pl.BlockSpec((1, d), const),
        ],
        out_specs=pl.BlockSpec((1, tm, d), lambda bi, i: (bi, i, 0)),
        out_shape=jax.ShapeDtypeStruct(hs.shape, F32),
        compiler_params=_params("parallel", "parallel"),
        name="convffn",
    )(hs, hs, nw2, w_up, conv_w, conv_b, w_down, nw3)


def _gdn_gates_col(graw, prm_ref):
    beta = jax.nn.sigmoid(graw)
    g = -jnp.exp(prm_ref[0:1, :]) * jax.nn.softplus(graw + prm_ref[1:2, :])
    return beta, g


def _gdn_prep_kernel(qk_ref, v_ref, qkh_ref, vh_ref, cwqk_ref, cwv_ref, gc_ref, gr_ref, prow_ref, ppair_ref,
                     a_ref, attn_ref, rhs_ref, qg_ref, kt_ref, gt_ref, qn_ref, kn_ref, vv_ref, *, nchunk):
    i = pl.program_id(1)
    rows = nchunk * CHUNK
    t = i * rows + lax.broadcasted_iota(jnp.int32, (rows, 1), 0)
    valid = (t >= N_PAD).astype(F32)
    first = i == 0

    def conv_silu(x_ref, halo_ref, cw_ref, col):
        sl = slice(col * GDN_HEAD_DIM, (col + 1) * GDN_HEAD_DIM)
        halo = jnp.where(first, 0.0, halo_ref[0, :, sl])
        x = jnp.concatenate([halo, x_ref[0, :, sl]], 0)
        cw = cw_ref[:, sl]
        y = (cw[3:4] * x[SUBLANES:]
             + cw[2:3] * pltpu.roll(x, 1, 0)[SUBLANES:]
             + cw[1:2] * pltpu.roll(x, 2, 0)[SUBLANES:]
             + cw[0:1] * pltpu.roll(x, 3, 0)[SUBLANES:])
        return _silu(y)

    def l2n(x):
        return x * lax.rsqrt(jnp.sum(x * x, -1, keepdims=True) + L2_EPS)

    dh = GDN_HEAD_DIM
    for hq in range(GDN_QK_HEADS):
        sl = slice(hq * dh, (hq + 1) * dh)
        qn_ref[:, sl] = l2n(conv_silu(qk_ref, qkh_ref, cwqk_ref, hq)) * (dh ** -0.5)
        kn_ref[:, sl] = l2n(conv_silu(qk_ref, qkh_ref, cwqk_ref, GDN_QK_HEADS + hq)) * valid
    for h in range(GDN_V_HEADS):
        sl = slice(h * dh, (h + 1) * dh)
        vv_ref[:, sl] = conv_silu(v_ref, vh_ref, cwv_ref, h) * valid

    r = lax.broadcasted_iota(jnp.int32, (CHUNK, LANES), 0)
    lane = lax.broadcasted_iota(jnp.int32, (CHUNK, LANES), 1)
    first_head = lane < CHUNK
    c = jnp.where(first_head, lane, lane - CHUNK)
    incl = r >= c
    strict = r > c
    r2 = lax.broadcasted_iota(jnp.int32, (LANES, LANES), 0)
    c2 = lax.broadcasted_iota(jnp.int32, (LANES, LANES), 1)
    cum_pair = ((r2 <= c2) & ((r2 < CHUNK) == (c2 < CHUNK))).astype(F32)
    rr, cc = _tri_incl()
    tri = (rr >= cc).astype(F32)
    nv = GDN_V_HEADS
    for ci in range(nchunk):
        rs = slice(ci * CHUNK, (ci + 1) * CHUNK)
        beta, g = _gdn_gates_col(gc_ref[0, rs, :], prow_ref)
        gcum = _dot_f32(tri, g)
        eg = jnp.exp(gcum)
        glast = gcum[CHUNK - 1:CHUNK, :]
        ktmul = jnp.exp(glast - gcum)
        egl = jnp.exp(glast)
        g_rp = -jnp.exp(ppair_ref[0]) * jax.nn.softplus(gr_ref[0, ci] + ppair_ref[1])
        gcum_rp = _dot_f32(g_rp, cum_pair)
        for hq in range(GDN_QK_HEADS):
            h0, h1 = 2 * hq, 2 * hq + 1
            qc = qn_ref[rs, hq * dh:(hq + 1) * dh]
            kc = kn_ref[rs, hq * dh:(hq + 1) * dh]
            kcb = kc.astype(BF16)
            k2 = jnp.concatenate([kcb, kcb], 0)
            beta_p = jnp.where(first_head, beta[:, h0:h0 + 1], beta[:, h1:h1 + 1])
            gi_p = jnp.where(first_head, gcum[:, nv + h0:nv + h0 + 1], gcum[:, nv + h1:nv + h1 + 1])
            dec_p = jnp.exp(jnp.where(incl, gi_p - gcum_rp[hq:hq + 1, :], -jnp.inf))
            a_ref[0, ci, hq] = jnp.where(strict, _dot_nt(kcb, k2) * beta_p * dec_p, 0.0)
            attn_p = (_dot_nt(qc.astype(BF16), k2) * dec_p).astype(BF16)
            attn_ref[0, ci, h0] = attn_p[:, :CHUNK]
            attn_ref[0, ci, h1] = attn_p[:, CHUNK:]
            for h in (h0, h1):
                beta_c = beta[:, h:h + 1]
                eg_c = eg[:, nv + h:nv + h + 1]
                hs = slice(h * dh, (h + 1) * dh)
                rhs_ref[0, rs, 2 * h * dh:(2 * h + 1) * dh] = (vv_ref[rs, hs] * beta_c).astype(BF16)
                rhs_ref[0, rs, (2 * h + 1) * dh:(2 * h + 2) * dh] = (kc * (beta_c * eg_c)).astype(BF16)
                qg_ref[0, rs, hs] = (qc * eg_c).astype(BF16)
                kt_ref[0, rs, hs] = (kc * ktmul[:, nv + h:nv + h + 1]).astype(BF16)
                gt_ref[0, ci, h:h + 1, :] = jnp.broadcast_to(egl[:, nv + h:nv + h + 1], (1, LANES))


def _gdn_prep(proj, gates, gates_rp, conv_w, prm_row, prm_pair, nchunk):
    b, seq_len, _ = proj.shape
    nc = seq_len // CHUNK
    rows = nchunk * CHUNK
    hb = rows // SUBLANES
    half = GDN_QKV_DIM // 2
    halo_map0 = lambda bi, i: (bi, jnp.maximum(i * hb - 1, 0), 0)
    halo_map1 = lambda bi, i: (bi, jnp.maximum(i * hb - 1, 0), 1)
    row_map = lambda bi, i: (bi, i, 0)
    return pl.pallas_call(
        functools.partial(_gdn_prep_kernel, nchunk=nchunk),
        grid=(b, nc // nchunk),
        in_specs=[
            pl.BlockSpec((1, rows, half), lambda bi, i: (bi, i, 0)),
            pl.BlockSpec((1, rows, half), lambda bi, i: (bi, i, 1)),
            pl.BlockSpec((1, SUBLANES, half), halo_map0),
            pl.BlockSpec((1, SUBLANES, half), halo_map1),
            pl.BlockSpec((GDN_CONV, half), lambda bi, i: (0, 0)),
            pl.BlockSpec((GDN_CONV, half), lambda bi, i: (0, 1)),
            pl.BlockSpec((1, rows, LANES), row_map),
            pl.BlockSpec((1, nchunk, GDN_QK_HEADS, LANES), lambda bi, i: (bi, i, 0, 0)),
            pl.BlockSpec((SUBLANES, LANES), lambda bi, i: (0, 0)),
            pl.BlockSpec((2, GDN_QK_HEADS, LANES), lambda bi, i: (0, 0, 0)),
        ],
        out_specs=[
            pl.BlockSpec((1, nchunk, GDN_QK_HEADS, CHUNK, LANES), lambda bi, i: (bi, i, 0, 0, 0)),
            pl.BlockSpec((1, nchunk, GDN_V_HEADS, CHUNK, CHUNK), lambda bi, i: (bi, i, 0, 0, 0)),
            pl.BlockSpec((1, rows, 2 * GDN_VAL_DIM), row_map),
            pl.BlockSpec((1, rows, GDN_VAL_DIM), row_map),
            pl.BlockSpec((1, rows, GDN_VAL_DIM), row_map),
            pl.BlockSpec((1, nchunk, GDN_V_HEADS, LANES), lambda bi, i: (bi, i, 0, 0)),
        ],
        out_shape=[
            jax.ShapeDtypeStruct((b, nc, GDN_QK_HEADS, CHUNK, LANES), F32),
            jax.ShapeDtypeStruct((b, nc, GDN_V_HEADS, CHUNK, CHUNK), BF16),
            jax.ShapeDtypeStruct((b, seq_len, 2 * GDN_VAL_DIM), BF16),
            jax.ShapeDtypeStruct((b, seq_len, GDN_VAL_DIM), BF16),
            jax.ShapeDtypeStruct((b, seq_len, GDN_VAL_DIM), BF16),
            jax.ShapeDtypeStruct((b, nc, GDN_V_HEADS, LANES), F32),
        ],
        scratch_shapes=[pltpu.VMEM((rows, GDN_KEY_DIM), F32), pltpu.VMEM((rows, GDN_KEY_DIM), F32),
                        pltpu.VMEM((rows, GDN_VAL_DIM), F32)],
        compiler_params=_params("parallel", "parallel"),
        name="gdn_prep",
    )(proj, proj, proj, proj, conv_w, conv_w, gates, gates_rp, prm_row, prm_pair)


def _tri_inv_kernel(a_ref, x_ref, at_scr, xt_scr):
    for i in range(CHUNK):
        at_scr[i] = a_ref[pl.ds(i, SOLVE_UNITS, stride=CHUNK), :].T

    xt_scr[...] = jnp.zeros_like(xt_scr)
    jrow = lax.broadcasted_iota(jnp.int32, (SUBLANES, SOLVE_UNITS), 0)
    sub = SUBLANES
    heads = LANES // CHUNK
    for ib in range(CHUNK // sub):
        def row_body(ii, carry, ib=ib):
            i = ib * sub + ii
            for e in range(heads):
                base = e * CHUNK
                acc = [(jrow + jb * sub == i).astype(F32) for jb in range(ib + 1)]
                for kb in range(ib + 1):
                    for kk in range(sub):
                        k = base + kb * sub + kk
                        a = at_scr[i, k:k + 1, :]
                        for jb in range(kb + 1):
                            acc[jb] = acc[jb] - a * xt_scr[kb * sub + kk, base + jb * sub:base + (jb + 1) * sub, :]
                for jb in range(ib + 1):
                    xt_scr[i, base + jb * sub:base + (jb + 1) * sub, :] = acc[jb]
            return carry

        lax.fori_loop(0, sub, row_body, 0)

    for i in range(CHUNK):
        x_ref[pl.ds(i, SOLVE_UNITS, stride=CHUNK), :] = xt_scr[i].T


def _tri_inv(a_rows):
    blk = (SOLVE_UNITS * CHUNK, LANES)
    return pl.pallas_call(
        _tri_inv_kernel,
        grid=(pl.cdiv(a_rows.shape[0], blk[0]),),
        in_specs=[pl.BlockSpec(blk, lambda i: (i, 0))],
        out_specs=pl.BlockSpec(blk, lambda i: (i, 0)),
        out_shape=jax.ShapeDtypeStruct(a_rows.shape, F32),
        scratch_shapes=[pltpu.VMEM((CHUNK, LANES, SOLVE_UNITS), F32),
                        pltpu.VMEM((CHUNK, LANES, SOLVE_UNITS), F32)],
        compiler_params=_params("parallel"),
        name="tri_inv",
    )(a_rows)


def _gdn_scan_kernel(rhs_ref, qg_ref, kt_ref, attn_ref, ti_ref, gt_ref, z_ref, nw_ref, o_ref, s_scr, *, nchunk):
    @pl.when(pl.program_id(1) == 0)
    def _():
        s_scr[...] = jnp.zeros_like(s_scr)

    rep = GDN_V_HEADS // GDN_QK_HEADS
    dh = GDN_HEAD_DIM

    def chunk_body(ci, carry):
        rs = pl.ds(pl.multiple_of(ci * CHUNK, CHUNK), CHUNK)
        pre = []
        for hq in range(GDN_QK_HEADS):
            tinv = ti_ref[0, ci, hq].astype(BF16)
            for e in range(rep):
                h = hq * rep + e
                sol = _dot(tinv[:, e * CHUNK:(e + 1) * CHUNK], rhs_ref[0, rs, 2 * h * dh:(2 * h + 2) * dh])
                u, w = sol[:, :dh], sol[:, dh:]
                pre.append((u, jnp.concatenate([w.astype(BF16), qg_ref[0, rs, h * dh:(h + 1) * dh]], 0)))
        mid = []
        for h in range(GDN_V_HEADS):
            u, wq = pre[h]
            ws_qs = _dot(wq, s_scr[h].astype(BF16))
            mid.append((ws_qs[CHUNK:], (u - ws_qs[:CHUNK]).astype(BF16)))
        for h in range(GDN_V_HEADS):
            qs, vb = mid[h]
            hs = slice(h * dh, (h + 1) * dh)
            o = qs + _dot(attn_ref[0, ci, h], vb)
            s_scr[h] = s_scr[h] * gt_ref[0, ci, h:h + 1, :] + _dot_tn(kt_ref[0, rs, hs], vb)
            o_ref[0, rs, hs] = (_rms(o, nw_ref[...]) * _silu(z_ref[0, rs, hs])).astype(BF16)
        return carry

    lax.fori_loop(0, nchunk, chunk_body, 0)


def _gdn_scan(rhs, qg, kt, attn, tinv, gt, proj, norm_w, nchunk):
    b, seq_len, _ = qg.shape
    nc = seq_len // CHUNK
    rows = nchunk * CHUNK
    zblk = GDN_QKV_DIM // GDN_VAL_DIM
    row_map = lambda bi, i: (bi, i, 0)
    return pl.pallas_call(
        functools.partial(_gdn_scan_kernel, nchunk=nchunk),
        grid=(b, nc // nchunk),
        in_specs=[
            pl.BlockSpec((1, rows, 2 * GDN_VAL_DIM), row_map),
            pl.BlockSpec((1, rows, GDN_VAL_DIM), row_map),
            pl.BlockSpec((1, rows, GDN_VAL_DIM), row_map),
            pl.BlockSpec((1, nchunk, GDN_V_HEADS, CHUNK, CHUNK), lambda bi, i: (bi, i, 0, 0, 0)),
            pl.BlockSpec((1, nchunk, GDN_QK_HEADS, CHUNK, LANES), lambda bi, i: (bi, i, 0, 0, 0)),
            pl.BlockSpec((1, nchunk, GDN_V_HEADS, LANES), lambda bi, i: (bi, i, 0, 0)),
            pl.BlockSpec((1, rows, GDN_VAL_DIM), lambda bi, i: (bi, i, zblk)),
            pl.BlockSpec((1, GDN_HEAD_DIM), lambda bi, i: (0, 0)),
        ],
        out_specs=pl.BlockSpec((1, rows, GDN_VAL_DIM), row_map),
        out_shape=jax.ShapeDtypeStruct((b, seq_len, GDN_VAL_DIM), BF16),
        scratch_shapes=[pltpu.VMEM((GDN_V_HEADS, GDN_HEAD_DIM, GDN_HEAD_DIM), F32)],
        compiler_params=_params("parallel", "arbitrary"),
        name="gdn_scan",
    )(rhs, qg, kt, attn, tinv, gt, proj, norm_w)


def _gated_deltanet(a2d, b, seq_len, nw_in, w_in, conv_w, a_log, dt_bias, norm_w, nchunk):
    nc = seq_len // CHUNK
    ng = 2 * GDN_V_HEADS
    w_main = w_in[:, :GDN_MAIN].astype(BF16)
    w_gate = jnp.pad(w_in[:, GDN_MAIN:], ((0, 0), (0, LANES - ng))).astype(BF16)
    proj, gates = _inproj(a2d, nw_in, w_main, w_gate, 1024)
    proj = proj.reshape(b, seq_len, GDN_MAIN)
    gates = gates.reshape(b, seq_len, LANES)
    assert (GDN_V_HEADS // GDN_QK_HEADS) * CHUNK == LANES
    gates_rp = jnp.swapaxes(gates[..., GDN_V_HEADS:ng].reshape(b, nc, CHUNK, GDN_V_HEADS), 2, 3)
    gates_rp = gates_rp.reshape(b, nc, GDN_QK_HEADS, LANES)
    pad16 = jnp.zeros((GDN_V_HEADS,), F32)
    prm_row = jnp.zeros((SUBLANES, LANES), F32)
    prm_row = prm_row.at[0, :ng].set(jnp.concatenate([pad16, a_log]))
    prm_row = prm_row.at[1, :ng].set(jnp.concatenate([pad16, dt_bias]))
    prm_pair = jnp.repeat(jnp.stack([a_log, dt_bias]).reshape(2, GDN_QK_HEADS, 2), CHUNK, axis=2)
    a_mat, attn, rhs, qg, kt, gt = _gdn_prep(proj, gates, gates_rp, conv_w, prm_row, prm_pair, nchunk)
    tinv = _tri_inv(a_mat.reshape(b * nc * GDN_QK_HEADS * CHUNK, LANES)).reshape(a_mat.shape)
    o = _gdn_scan(rhs, qg, kt, attn, tinv, gt, proj, norm_w.reshape(1, GDN_HEAD_DIM), nchunk)
    return o.reshape(b * seq_len, GDN_VAL_DIM)


def _mlstm_kernel(q_ref, k_ref, v_ref, og_ref, gc_ref, gr_ref, brow_ref, bcol_ref, nw_ref, o_ref,
                  c_scr, n_scr, m_scr, *, nchunk):
    i = pl.program_id(0)

    @pl.when(i == 0)
    def _():
        c_scr[...] = jnp.zeros_like(c_scr)
        n_scr[...] = jnp.zeros_like(n_scr)
        m_scr[...] = jnp.zeros_like(m_scr)

    r, c = _tri_incl()
    incl = r >= c
    tri = incl.astype(F32)
    tri_t = (r <= c).astype(F32)
    dk, dv = ML_QK_DIM, ML_V_DIM
    nb = q_ref.shape[0]
    chains = [(bi, h) for bi in range(nb) for h in range(ML_HEADS)]

    def chunk_body(ci, carry):
        rs = pl.ds(pl.multiple_of(ci * CHUNK, CHUNK), CHUNK)
        t = (i * nchunk + ci) * CHUNK + lax.broadcasted_iota(jnp.int32, (CHUNK, 1), 0)
        valid = (t >= N_PAD).astype(F32)
        gate = {}
        for bi in range(nb):
            gcap = ML_GATE_CAP * jnp.tanh((gc_ref[bi, rs, :] + brow_ref[0:1, :]) / ML_GATE_CAP)
            gcap_r = ML_GATE_CAP * jnp.tanh((gr_ref[bi, ci] + bcol_ref[:, 0:1]) / ML_GATE_CAP)
            gate[bi] = (gcap, _dot_f32(tri, jax.nn.log_sigmoid(gcap)),
                        gcap_r, _dot_f32(jax.nn.log_sigmoid(gcap_r), tri_t))
        st = {ch: {} for ch in chains}
        for bi, h in chains:
            s = st[bi, h]
            s["q"] = q_ref[bi, rs, h * dk:(h + 1) * dk] * (dk ** -0.5)
            s["k"] = k_ref[bi, rs, h * dk:(h + 1) * dk] * valid
            s["qb"] = s["q"].astype(BF16)
            s["vb"] = (v_ref[bi, rs, h * dv:(h + 1) * dv] * valid).astype(BF16)
            s["qk"] = _dot_nt(s["qb"], s["k"].astype(BF16))
        for bi, h in chains:
            s = st[bi, h]
            gcap, bcum, gcap_r, bcum_r = gate[bi]
            b_i = bcum[:, ML_HEADS + h:ML_HEADS + h + 1]
            s["bl"] = b_i[CHUNK - 1:CHUNK, :]
            s["ms"] = m_scr[bi * ML_HEADS + h]
            s["a"] = s["bl"] - b_i + gcap[:, h:h + 1]
            s["inter"] = b_i + s["ms"]
            s["d"] = jnp.where(incl, b_i - bcum_r[ML_HEADS + h:ML_HEADS + h + 1, :] + gcap_r[h:h + 1, :], -jnp.inf)
        for ch in chains:
            s = st[ch]
            s["dmax"] = jnp.max(s["d"], -1, keepdims=True)
            s["amax"] = jnp.max(s["a"], 0, keepdims=True)
        for ch in chains:
            s = st[ch]
            s["m_t"] = jnp.maximum(s["dmax"], s["inter"])
            s["m_new"] = jnp.maximum(s["bl"] + s["ms"], s["amax"])
        for ch in chains:
            s = st[ch]
            s["kw"] = s["k"] * jnp.exp(s["a"] - s["m_new"])
            s["w_d"] = jnp.exp(s["d"] - s["m_t"]) * s["qk"]
        for ch in chains:
            s = st[ch]
            s["kv"] = _dot_tn(s["kw"].astype(BF16), s["vb"])
            s["wv"] = _dot(s["w_d"].astype(BF16), s["vb"])
        for bi, h in chains:
            s = st[bi, h]
            s["qc"] = _dot(s["qb"], c_scr[bi * ML_HEADS + h].astype(BF16))
            s["qn"] = jnp.sum(s["q"] * n_scr[bi * ML_HEADS + h], -1, keepdims=True)
            s["wsum"] = jnp.sum(s["w_d"], -1, keepdims=True)
        for ch in chains:
            s = st[ch]
            sc = jnp.exp(s["inter"] - s["m_t"])
            den = sc * s["qn"] + s["wsum"]
            s["hh"] = (sc * s["qc"] + s["wv"]) / jnp.maximum(jnp.abs(den), jnp.exp(-s["m_t"]))
        for bi, h in chains:
            s = st[bi, h]
            si = bi * ML_HEADS + h
            dec = jnp.exp(s["bl"] + s["ms"] - s["m_new"])
            c_scr[si] = dec * c_scr[si] + s["kv"]
            n_scr[si] = dec * n_scr[si] + jnp.sum(s["kw"], 0, keepdims=True)
            m_scr[si] = s["m_new"]
            vs = slice(h * dv, (h + 1) * dv)
            o_ref[bi, rs, vs] = (_rms(s["hh"], nw_ref[:, vs]) * jax.nn.sigmoid(og_ref[bi, rs, vs])).astype(BF16)
        return carry

    lax.fori_loop(0, nchunk, chunk_body, 0)


def _mlstm(a2d, b, seq_len, nw_in, w_in, gate_b, norm_w, nchunk):
    nc = seq_len // CHUNK
    ng = 2 * ML_HEADS
    w_main = w_in[:, :ML_MAIN].astype(BF16)
    w_gate = jnp.pad(w_in[:, ML_MAIN:], ((0, 0), (0, LANES - ng))).astype(BF16)
    proj, gates = _inproj(a2d, nw_in, w_main, w_gate, 1024)
    proj = proj.reshape(b, seq_len, ML_MAIN)
    gates = gates.reshape(b, seq_len, LANES)
    gates_r = jnp.swapaxes(gates[..., :ng].reshape(b, nc, CHUNK, ng), 2, 3)
    b_row = jnp.zeros((SUBLANES, LANES), F32).at[0, :ng].set(gate_b)
    b_col = jnp.zeros((ng, LANES), F32).at[:, 0].set(gate_b)
    rows = nchunk * CHUNK
    qkb = ML_QK_TOT
    o = pl.pallas_call(
        functools.partial(_mlstm_kernel, nchunk=nchunk),
        grid=(nc // nchunk,),
        in_specs=[
            pl.BlockSpec((b, rows, qkb), lambda i: (0, i, 0)),
            pl.BlockSpec((b, rows, qkb), lambda i: (0, i, 1)),
            pl.BlockSpec((b, rows, ML_V_TOT), lambda i: (0, i, 1)),
            pl.BlockSpec((b, rows, ML_V_TOT), lambda i: (0, i, 2)),
            pl.BlockSpec((b, rows, LANES), lambda i: (0, i, 0)),
            pl.BlockSpec((b, nchunk, ng, CHUNK), lambda i: (0, i, 0, 0)),
            pl.BlockSpec((SUBLANES, LANES), lambda i: (0, 0)),
            pl.BlockSpec((ng, LANES), lambda i: (0, 0)),
            pl.BlockSpec((1, ML_V_TOT), lambda i: (0, 0)),
        ],
        out_specs=pl.BlockSpec((b, rows, ML_V_TOT), lambda i: (0, i, 0)),
        out_shape=jax.ShapeDtypeStruct((b, seq_len, ML_V_TOT), BF16),
        scratch_shapes=[pltpu.VMEM((b * ML_HEADS, ML_QK_DIM, ML_V_DIM), F32),
                        pltpu.VMEM((b * ML_HEADS, 1, ML_QK_DIM), F32),
                        pltpu.VMEM((b * ML_HEADS, 1, 1), F32)],
        compiler_params=_params("arbitrary"),
        name="mlstm",
    )(proj, proj, proj, proj, gates, gates_r, b_row, b_col, norm_w.reshape(1, ML_V_TOT))
    return o.reshape(b * seq_len, ML_V_TOT)


def kernel(x, meta_tokens, norm_w, gdn_w_in, gdn_conv_w, gdn_a_log, gdn_dt_bias, gdn_norm_w, gdn_w_out,
           ml_w_in, ml_gate_b, ml_norm_w, ml_w_out, ffn_w_up, ffn_conv_w, ffn_conv_b, ffn_w_down):
    b, s, d = x.shape
    depth = norm_w.shape[0]
    pad = jnp.zeros((b, N_PAD, d), x.dtype)
    meta = jnp.broadcast_to(meta_tokens[None].astype(x.dtype), (b, N_META, d))
    hs = jnp.concatenate([pad, meta, x], 1)
    seq_len = hs.shape[1]
    nc = seq_len // CHUNK
    nchunk = 3 if nc % 3 == 0 else 1
    for i in range(depth):
        j = i // 2
        nw = norm_w[i].reshape(4, 1, d)
        hs2d = hs.reshape(b * seq_len, d)
        if i % 2 == 0:
            mix = _gated_deltanet(hs2d, b, seq_len, nw[0], gdn_w_in[j], gdn_conv_w[j], gdn_a_log[j],
                                  gdn_dt_bias[j], gdn_norm_w[j], nchunk)
            w_out = gdn_w_out[j]
        else:
            mix = _mlstm(hs2d, b, seq_len, nw[0], ml_w_in[j], ml_gate_b[j], ml_norm_w[j], nchunk)
            w_out = ml_w_out[j]
        hs = _outproj(mix, w_out.astype(BF16), nw[1], hs2d, seq_len).reshape(b, seq_len, d)
        hs = _ffn(hs, nw[2], ffn_w_up[i].astype(BF16), ffn_conv_w[i], ffn_conv_b[i][None],
                  ffn_w_down[i].astype(BF16), nw[3])
    return hs[:, N_PAD + N_META:]
```

```python
import functools

import jax
import jax.numpy as jnp
from jax import lax
from jax.experimental import pallas as pl
from jax.experimental.pallas import tpu as pltpu

F32 = jnp.float32
BF16 = jnp.bfloat16

CHUNK = 64
N_META = 16
N_PAD = CHUNK - N_META
RMS_EPS = 1e-6
L2_EPS = 1e-6

GDN_QK_HEADS = 8
GDN_V_HEADS = 16
GDN_HEAD_DIM = 128
GDN_CONV = 4
GDN_KEY_DIM = GDN_QK_HEADS * GDN_HEAD_DIM
GDN_VAL_DIM = GDN_V_HEADS * GDN_HEAD_DIM
GDN_QKV_DIM = 2 * GDN_KEY_DIM + GDN_VAL_DIM
GDN_MAIN = GDN_QKV_DIM + GDN_VAL_DIM

ML_HEADS = 4
ML_QK_DIM = 128
ML_V_DIM = 256
ML_GATE_CAP = 15.0
ML_QK_TOT = ML_HEADS * ML_QK_DIM
ML_V_TOT = ML_HEADS * ML_V_DIM
ML_MAIN = 2 * ML_QK_TOT + 2 * ML_V_TOT

FFN_CONV = 3
FFN_BLOCK = 256

LANES = 128
SUBLANES = 8
SOLVE_UNITS = 128
VMEM_LIMIT = 56 * 1024 * 1024


def _largest_tile(n, cap):
    d = (min(cap, n) // SUBLANES) * SUBLANES
    while d > SUBLANES and n % d:
        d -= SUBLANES
    assert d >= SUBLANES and n % d == 0, (n, cap)
    return d


def _params(*sem):
    return pltpu.CompilerParams(dimension_semantics=sem, vmem_limit_bytes=VMEM_LIMIT)


def _rms(x, w):
    return x * lax.rsqrt(jnp.mean(x * x, -1, keepdims=True) + RMS_EPS) * w


def _silu(x):
    return x * jax.nn.sigmoid(x)


def _dot(a, b):
    return jnp.dot(a, b, preferred_element_type=F32)


def _dot_nt(a, b):
    return lax.dot_general(a, b, (((1,), (1,)), ((), ())), preferred_element_type=F32)


def _dot_tn(a, b):
    return lax.dot_general(a, b, (((0,), (0,)), ((), ())), preferred_element_type=F32)


def _dot_f32(a, b):
    return jnp.dot(a, b, preferred_element_type=F32, precision=lax.Precision.HIGHEST)


def _tri_incl():
    r = lax.broadcasted_iota(jnp.int32, (CHUNK, CHUNK), 0)
    c = lax.broadcasted_iota(jnp.int32, (CHUNK, CHUNK), 1)
    return r, c


def _inproj_kernel(x_ref, nw_ref, wm_ref, wg_ref, om_ref, og_ref, a_scr):
    @pl.when(pl.program_id(1) == 0)
    def _():
        a = _rms(x_ref[...], nw_ref[...]).astype(BF16)
        a_scr[...] = a
        og_ref[...] = _dot(a, wg_ref[...])

    om_ref[...] = _dot(a_scr[...], wm_ref[...])


def _inproj(x2d, nw, w_main, w_gate, tn):
    m, d = x2d.shape
    n = w_main.shape[1]
    tm = _largest_tile(m, 1400)
    return pl.pallas_call(
        _inproj_kernel,
        grid=(m // tm, n // tn),
        in_specs=[
            pl.BlockSpec((tm, d), lambda i, j: (i, 0)),
            pl.BlockSpec((1, d), lambda i, j: (0, 0)),
            pl.BlockSpec((d, tn), lambda i, j: (0, j)),
            pl.BlockSpec((d, LANES), lambda i, j: (0, 0)),
        ],
        out_specs=[
            pl.BlockSpec((tm, tn), lambda i, j: (i, j)),
            pl.BlockSpec((tm, LANES), lambda i, j: (i, 0)),
        ],
        out_shape=[jax.ShapeDtypeStruct((m, n), F32), jax.ShapeDtypeStruct((m, LANES), F32)],
        scratch_shapes=[pltpu.VMEM((tm, d), BF16)],
        compiler_params=_params("parallel", "arbitrary"),
        name="inproj",
    )(x2d, nw, w_main, w_gate)


def _outproj_kernel(o_ref, w_ref, nw_ref, hs_ref, out_ref, *, tm, seq_len):
    y = _rms(_dot(o_ref[...], w_ref[...]), nw_ref[...])
    t = (pl.program_id(0) * tm) % seq_len + lax.broadcasted_iota(jnp.int32, (tm, 1), 0)
    out_ref[...] = hs_ref[...] + jnp.where(t >= N_PAD, y, 0.0)


def _outproj(o2d, w_out, nw, hs2d, seq_len):
    m, kin = o2d.shape
    d = hs2d.shape[1]
    tm = _largest_tile(seq_len, 1400)
    return pl.pallas_call(
        functools.partial(_outproj_kernel, tm=tm, seq_len=seq_len),
        grid=(m // tm,),
        in_specs=[
            pl.BlockSpec((tm, kin), lambda i: (i, 0)),
            pl.BlockSpec((kin, d), lambda i: (0, 0)),
            pl.BlockSpec((1, d), lambda i: (0, 0)),
            pl.BlockSpec((tm, d), lambda i: (i, 0)),
        ],
        out_specs=pl.BlockSpec((tm, d), lambda i: (i, 0)),
        out_shape=jax.ShapeDtypeStruct((m, d), F32),
        compiler_params=_params("parallel"),
        name="outproj",
    )(o2d, w_out, nw, hs2d)


def _ffn_kernel(h_ref, halo_ref, nw2_ref, wup_ref, cw_ref, cb_ref, wdn_ref, nw3_ref, out_ref, *, tm):
    i = pl.program_id(1)
    h = h_ref[0]
    halo = jnp.where(i == 0, 0.0, halo_ref[0])
    a = _rms(jnp.concatenate([halo, h], 0), nw2_ref[...]).astype(BF16)
    half = tm // 2
    a_lo, a_hi = a[:half + SUBLANES], a[half:]
    fdim = wdn_ref.shape[0]
    cols = [(slice(j * FFN_BLOCK, (j + 1) * FFN_BLOCK), slice(fdim + j * FFN_BLOCK, fdim + (j + 1) * FFN_BLOCK))
            for j in range(fdim // FFN_BLOCK)]

    def up(a_x, j):
        return [_dot(a_x, wup_ref[:, sl]) for sl in cols[j]]

    def conv(u, sl):
        cw = cw_ref[:, sl]
        return (cw[2:3] * u[SUBLANES:]
                + cw[1:2] * pltpu.roll(u, 1, 0)[SUBLANES:]
                + cw[0:1] * pltpu.roll(u, 2, 0)[SUBLANES:]
                + cb_ref[:, sl])

    def down(u, j):
        act = (_silu(conv(u[0], cols[j][0])) * conv(u[1], cols[j][1])).astype(BF16)
        return _dot(act, wdn_ref[cols[j][0], :])

    acc_lo = acc_hi = 0.0
    u_lo = up(a_lo, 0)
    for j in range(len(cols)):
        u_hi = up(a_hi, j)
        acc_lo = acc_lo + down(u_lo, j)
        if j + 1 < len(cols):
            u_lo = up(a_lo, j + 1)
        acc_hi = acc_hi + down(u_hi, j)
    f = jnp.concatenate([acc_lo, acc_hi], 0)
    t = i * tm + lax.broadcasted_iota(jnp.int32, (tm, 1), 0)
    out_ref[0] = h + jnp.where(t >= N_PAD, _rms(f, nw3_ref[...]), 0.0)


def _ffn(hs, nw2, w_up, conv_w, conv_b, w_down, nw3):
    b, seq_len, d = hs.shape
    tm = _largest_tile(seq_len, 700)
    hb = tm // SUBLANES
    assert tm % (2 * SUBLANES) == 0 and w_down.shape[0] % FFN_BLOCK == 0
    const = lambda bi, i: (0, 0)
    return pl.pallas_call(
        functools.partial(_ffn_kernel, tm=tm),
        grid=(b, seq_len // tm),
        in_specs=[
            pl.BlockSpec((1, tm, d), lambda bi, i: (bi, i, 0)),
            pl.BlockSpec((1, SUBLANES, d), lambda bi, i: (bi, jnp.maximum(i * hb - 1, 0), 0)),
            pl.BlockSpec((1, d), const),
            pl.BlockSpec(w_up.shape, const),
            pl.BlockSpec(conv_w.shape, const),
            pl.BlockSpec(conv_b.shape, const),
            pl.BlockSpec(w_down.shape, const),
            pl.BlockSpec((1, d), const),
        ],
        out_specs=pl.BlockSpec((1, tm, d), lambda bi, i: (bi, i, 0)),
        out_shape=jax.ShapeDtypeStruct(hs.shape, F32),
        compiler_params=_params("parallel", "parallel"),
        name="convffn",
    )(hs, hs, nw2, w_up, conv_w, conv_b, w_down, nw3)


def _gdn_gates_col(graw, prm_ref):
    beta = jax.nn.sigmoid(graw)
    g = -jnp.exp(prm_ref[0:1, :]) * jax.nn.softplus(graw + prm_ref[1:2, :])
    return beta, g


def _gdn_prep_kernel(qk_ref, v_ref, qkh_ref, vh_ref, cwqk_ref, cwv_ref, gc_ref, gr_ref, prow_ref, ppair_ref,
                     a_ref, attn_ref, rhs_ref, qg_ref, kt_ref, gt_ref, qn_ref, kn_ref, vv_ref, *, nchunk):
    i = pl.program_id(1)
    rows = nchunk * CHUNK
    t = i * rows + lax.broadcasted_iota(jnp.int32, (rows, 1), 0)
    valid = (t >= N_PAD).astype(F32)
    first = i == 0

    def conv_silu(x_ref, halo_ref, cw_ref, col):
        sl = slice(col * GDN_HEAD_DIM, (col + 1) * GDN_HEAD_DIM)
        halo = jnp.where(first, 0.0, halo_ref[0, :, sl])
        x = jnp.concatenate([halo, x_ref[0, :, sl]], 0)
        cw = cw_ref[:, sl]
        y = (cw[3:4] * x[SUBLANES:]
             + cw[2:3] * pltpu.roll(x, 1, 0)[SUBLANES:]
             + cw[1:2] * pltpu.roll(x, 2, 0)[SUBLANES:]
             + cw[0:1] * pltpu.roll(x, 3, 0)[SUBLANES:])
        return _silu(y)

    dh = GDN_HEAD_DIM
    nq = GDN_QK_HEADS
    for h in range(GDN_V_HEADS):
        sl = slice(h * dh, (h + 1) * dh)
        vv_ref[:, sl] = conv_silu(v_ref, vh_ref, cwv_ref, h) * valid
    qk_raw = [conv_silu(qk_ref, qkh_ref, cwqk_ref, col) for col in range(2 * nq)]
    qk_inv = [lax.rsqrt(jnp.sum(x * x, -1, keepdims=True) + L2_EPS) for x in qk_raw]
    for hq in range(nq):
        sl = slice(hq * dh, (hq + 1) * dh)
        qn_ref[:, sl] = qk_raw[hq] * qk_inv[hq] * (dh ** -0.5)
        kn_ref[:, sl] = qk_raw[nq + hq] * qk_inv[nq + hq] * valid

    r = lax.broadcasted_iota(jnp.int32, (CHUNK, LANES), 0)
    lane = lax.broadcasted_iota(jnp.int32, (CHUNK, LANES), 1)
    first_head = lane < CHUNK
    c = jnp.where(first_head, lane, lane - CHUNK)
    incl = r >= c
    strict = r > c
    r2 = lax.broadcasted_iota(jnp.int32, (LANES, LANES), 0)
    c2 = lax.broadcasted_iota(jnp.int32, (LANES, LANES), 1)
    cum_pair = ((r2 <= c2) & ((r2 < CHUNK) == (c2 < CHUNK))).astype(F32)
    rr, cc = _tri_incl()
    tri = (rr >= cc).astype(F32)
    nv = GDN_V_HEADS
    for ci in range(nchunk):
        rs = slice(ci * CHUNK, (ci + 1) * CHUNK)
        beta, g = _gdn_gates_col(gc_ref[0, rs, :], prow_ref)
        gcum = _dot_f32(tri, g)
        eg = jnp.exp(gcum)
        glast = gcum[CHUNK - 1:CHUNK, :]
        ktmul = jnp.exp(glast - gcum)
        egl = jnp.exp(glast)
        g_rp = -jnp.exp(ppair_ref[0]) * jax.nn.softplus(gr_ref[0, ci] + ppair_ref[1])
        gcum_rp = _dot_f32(g_rp, cum_pair)
        beg = beta * pltpu.roll(eg, LANES - nv, 1)
        qs = [qn_ref[rs, hq * dh:(hq + 1) * dh] for hq in range(nq)]
        ks = [kn_ref[rs, hq * dh:(hq + 1) * dh] for hq in range(nq)]
        k2 = [jnp.concatenate([k.astype(BF16)] * 2, 0) for k in ks]
        kk = [_dot_nt(k2[hq][:CHUNK], k2[hq]) for hq in range(nq)]
        qk = [_dot_nt(qs[hq].astype(BF16), k2[hq]) for hq in range(nq)]
        col = lambda x, j: jnp.broadcast_to(x[:, j:j + 1], (CHUNK, LANES))
        beta_b = [col(beta, h) for h in range(nv)]
        gi_b = [col(gcum, nv + h) for h in range(nv)]
        for hq in range(nq):
            h0, h1 = 2 * hq, 2 * hq + 1
            dec_p = jnp.exp(jnp.where(incl, jnp.where(first_head, gi_b[h0], gi_b[h1]) - gcum_rp[hq:hq + 1, :],
                                      -jnp.inf))
            a_ref[0, ci, hq] = jnp.where(strict, kk[hq] * jnp.where(first_head, beta_b[h0], beta_b[h1]) * dec_p, 0.0)
            attn_p = (qk[hq] * dec_p).astype(BF16)
            attn_ref[0, ci, h0] = attn_p[:, :CHUNK]
            attn_ref[0, ci, h1] = attn_p[:, CHUNK:]
        beg_b = [col(beg, h) for h in range(nv)]
        eg_b = [col(eg, nv + h) for h in range(nv)]
        kt_b = [col(ktmul, nv + h) for h in range(nv)]
        for h in range(nv):
            hs = slice(h * dh, (h + 1) * dh)
            rhs_ref[0, rs, 2 * h * dh:(2 * h + 1) * dh] = (vv_ref[rs, hs] * beta_b[h]).astype(BF16)
            rhs_ref[0, rs, (2 * h + 1) * dh:(2 * h + 2) * dh] = (ks[h // 2] * beg_b[h]).astype(BF16)
            qg_ref[0, rs, hs] = (qs[h // 2] * eg_b[h]).astype(BF16)
            kt_ref[0, rs, hs] = (ks[h // 2] * kt_b[h]).astype(BF16)
            gt_ref[0, ci, h:h + 1, :] = jnp.broadcast_to(egl[:, nv + h:nv + h + 1], (1, LANES))


def _gdn_prep(proj, gates, gates_rp, conv_w, prm_row, prm_pair, nchunk):
    b, seq_len, _ = proj.shape
    nc = seq_len // CHUNK
    rows = nchunk * CHUNK
    hb = rows // SUBLANES
    half = GDN_QKV_DIM // 2
    halo_map0 = lambda bi, i: (bi, jnp.maximum(i * hb - 1, 0), 0)
    halo_map1 = lambda bi, i: (bi, jnp.maximum(i * hb - 1, 0), 1)
    row_map = lambda bi, i: (bi, i, 0)
    return pl.pallas_call(
        functools.partial(_gdn_prep_kernel, nchunk=nchunk),
        grid=(b, nc // nchunk),
        in_specs=[
            pl.BlockSpec((1, rows, half), lambda bi, i: (bi, i, 0)),
            pl.BlockSpec((1, rows, half), lambda bi, i: (bi, i, 1)),
            pl.BlockSpec((1, SUBLANES, half), halo_map0),
            pl.BlockSpec((1, SUBLANES, half), halo_map1),
            pl.BlockSpec((GDN_CONV, half), lambda bi, i: (0, 0)),
            pl.BlockSpec((GDN_CONV, half), lambda bi, i: (0, 1)),
            pl.BlockSpec((1, rows, LANES), row_map),
            pl.BlockSpec((1, nchunk, GDN_QK_HEADS, LANES), lambda bi, i: (bi, i, 0, 0)),
            pl.BlockSpec((SUBLANES, LANES), lambda bi, i: (0, 0)),
            pl.BlockSpec((2, GDN_QK_HEADS, LANES), lambda bi, i: (0, 0, 0)),
        ],
        out_specs=[
            pl.BlockSpec((1, nchunk, GDN_QK_HEADS, CHUNK, LANES), lambda bi, i: (bi, i, 0, 0, 0)),
            pl.BlockSpec((1, nchunk, GDN_V_HEADS, CHUNK, CHUNK), lambda bi, i: (bi, i, 0, 0, 0)),
            pl.BlockSpec((1, rows, 2 * GDN_VAL_DIM), row_map),
            pl.BlockSpec((1, rows, GDN_VAL_DIM), row_map),
            pl.BlockSpec((1, rows, GDN_VAL_DIM), row_map),
            pl.BlockSpec((1, nchunk, GDN_V_HEADS, LANES), lambda bi, i: (bi, i, 0, 0)),
        ],
        out_shape=[
            jax.ShapeDtypeStruct((b, nc, GDN_QK_HEADS, CHUNK, LANES), F32),
            jax.ShapeDtypeStruct((b, nc, GDN_V_HEADS, CHUNK, CHUNK), BF16),
            jax.ShapeDtypeStruct((b, seq_len, 2 * GDN_VAL_DIM), BF16),
            jax.ShapeDtypeStruct((b, seq_len, GDN_VAL_DIM), BF16),
            jax.ShapeDtypeStruct((b, seq_len, GDN_VAL_DIM), BF16),
            jax.ShapeDtypeStruct((b, nc, GDN_V_HEADS, LANES), F32),
        ],
        scratch_shapes=[pltpu.VMEM((rows, GDN_KEY_DIM), F32), pltpu.VMEM((rows, GDN_KEY_DIM), F32),
                        pltpu.VMEM((rows, GDN_VAL_DIM), F32)],
        compiler_params=_params("parallel", "parallel"),
        name="gdn_prep",
    )(proj, proj, proj, proj, conv_w, conv_w, gates, gates_rp, prm_row, prm_pair)


def _tri_inv_kernel(a_ref, x_ref, at_scr, xt_scr):
    for i in range(CHUNK):
        at_scr[i] = a_ref[pl.ds(i, SOLVE_UNITS, stride=CHUNK), :].T

    xt_scr[...] = jnp.zeros_like(xt_scr)
    jrow = lax.broadcasted_iota(jnp.int32, (SUBLANES, SOLVE_UNITS), 0)
    sub = SUBLANES
    heads = LANES // CHUNK
    for ib in range(CHUNK // sub):
        def row_body(ii, carry, ib=ib):
            i = ib * sub + ii
            for e in range(heads):
                base = e * CHUNK
                acc = [(jrow + jb * sub == i).astype(F32) for jb in range(ib + 1)]
                for kb in range(ib + 1):
                    for kk in range(sub):
                        k = base + kb * sub + kk
                        a = at_scr[i, k:k + 1, :]
                        for jb in range(kb + 1):
                            acc[jb] = acc[jb] - a * xt_scr[kb * sub + kk, base + jb * sub:base + (jb + 1) * sub, :]
                for jb in range(ib + 1):
                    xt_scr[i, base + jb * sub:base + (jb + 1) * sub, :] = acc[jb]
            return carry

        lax.fori_loop(0, sub, row_body, 0)

    for i in range(CHUNK):
        x_ref[pl.ds(i, SOLVE_UNITS, stride=CHUNK), :] = xt_scr[i].T


def _tri_inv(a_rows):
    blk = (SOLVE_UNITS * CHUNK, LANES)
    return pl.pallas_call(
        _tri_inv_kernel,
        grid=(pl.cdiv(a_rows.shape[0], blk[0]),),
        in_specs=[pl.BlockSpec(blk, lambda i: (i, 0))],
        out_specs=pl.BlockSpec(blk, lambda i: (i, 0)),
        out_shape=jax.ShapeDtypeStruct(a_rows.shape, F32),
        scratch_shapes=[pltpu.VMEM((CHUNK, LANES, SOLVE_UNITS), F32),
                        pltpu.VMEM((CHUNK, LANES, SOLVE_UNITS), F32)],
        compiler_params=_params("parallel"),
        name="tri_inv",
    )(a_rows)


def _gdn_scan_kernel(rhs_ref, qg_ref, kt_ref, attn_ref, ti_ref, gt_ref, z_ref, nw_ref, o_ref, s_scr, *, nchunk):
    @pl.when(pl.program_id(1) == 0)
    def _():
        s_scr[...] = jnp.zeros_like(s_scr)

    rep = GDN_V_HEADS // GDN_QK_HEADS
    dh = GDN_HEAD_DIM

    def chunk_body(ci, carry):
        rs = pl.ds(pl.multiple_of(ci * CHUNK, CHUNK), CHUNK)
        pre = []
        for hq in range(GDN_QK_HEADS):
            tinv = ti_ref[0, ci, hq].astype(BF16)
            for e in range(rep):
                h = hq * rep + e
                sol = _dot(tinv[:, e * CHUNK:(e + 1) * CHUNK], rhs_ref[0, rs, 2 * h * dh:(2 * h + 2) * dh])
                u, w = sol[:, :dh], sol[:, dh:]
                pre.append((u, jnp.concatenate([w.astype(BF16), qg_ref[0, rs, h * dh:(h + 1) * dh]], 0)))
        mid = []
        for h in range(GDN_V_HEADS):
            u, wq = pre[h]
            ws_qs = _dot(wq, s_scr[h].astype(BF16))
            mid.append((ws_qs[CHUNK:], (u - ws_qs[:CHUNK]).astype(BF16)))
        for h in range(GDN_V_HEADS):
            qs, vb = mid[h]
            hs = slice(h * dh, (h + 1) * dh)
            o = qs + _dot(attn_ref[0, ci, h], vb)
            s_scr[h] = s_scr[h] * gt_ref[0, ci, h:h + 1, :] + _dot_tn(kt_ref[0, rs, hs], vb)
            o_ref[0, rs, hs] = (_rms(o, nw_ref[...]) * _silu(z_ref[0, rs, hs])).astype(BF16)
        return carry

    lax.fori_loop(0, nchunk, chunk_body, 0)


def _gdn_scan(rhs, qg, kt, attn, tinv, gt, proj, norm_w, nchunk):
    b, seq_len, _ = qg.shape
    nc = seq_len // CHUNK
    rows = nchunk * CHUNK
    zblk = GDN_QKV_DIM // GDN_VAL_DIM
    row_map = lambda bi, i: (bi, i, 0)
    return pl.pallas_call(
        functools.partial(_gdn_scan_kernel, nchunk=nchunk),
        grid=(b, nc // nchunk),
        in_specs=[
            pl.BlockSpec((1, rows, 2 * GDN_VAL_DIM), row_map),
            pl.BlockSpec((1, rows, GDN_VAL_DIM), row_map),
            pl.BlockSpec((1, rows, GDN_VAL_DIM), row_map),
            pl.BlockSpec((1, nchunk, GDN_V_HEADS, CHUNK, CHUNK), lambda bi, i: (bi, i, 0, 0, 0)),
            pl.BlockSpec((1, nchunk, GDN_QK_HEADS, CHUNK, LANES), lambda bi, i: (bi, i, 0, 0, 0)),
            pl.BlockSpec((1, nchunk, GDN_V_HEADS, LANES), lambda bi, i: (bi, i, 0, 0)),
            pl.BlockSpec((1, rows, GDN_VAL_DIM), lambda bi, i: (bi, i, zblk)),
            pl.BlockSpec((1, GDN_HEAD_DIM), lambda bi, i: (0, 0)),
        ],
        out_specs=pl.BlockSpec((1, rows, GDN_VAL_DIM), row_map),
        out_shape=jax.ShapeDtypeStruct((b, seq_len, GDN_VAL_DIM), BF16),
        scratch_shapes=[pltpu.VMEM((GDN_V_HEADS, GDN_HEAD_DIM, GDN_HEAD_DIM), F32)],
        compiler_params=_params("parallel", "arbitrary"),
        name="gdn_scan",
    )(rhs, qg, kt, attn, tinv, gt, proj, norm_w)


def _gated_deltanet(a2d, b, seq_len, nw_in, w_in, conv_w, a_log, dt_bias, norm_w, nchunk):
    nc = seq_len // CHUNK
    ng = 2 * GDN_V_HEADS
    w_main = w_in[:, :GDN_MAIN].astype(BF16)
    w_gate = jnp.pad(w_in[:, GDN_MAIN:], ((0, 0), (0, LANES - ng))).astype(BF16)
    proj, gates = _inproj(a2d, nw_in, w_main, w_gate, 1024)
    proj = proj.reshape(b, seq_len, GDN_MAIN)
    gates = gates.reshape(b, seq_len, LANES)
    assert (GDN_V_HEADS // GDN_QK_HEADS) * CHUNK == LANES
    gates_rp = jnp.swapaxes(gates[..., GDN_V_HEADS:ng].reshape(b, nc, CHUNK, GDN_V_HEADS), 2, 3)
    gates_rp = gates_rp.reshape(b, nc, GDN_QK_HEADS, LANES)
    pad16 = jnp.zeros((GDN_V_HEADS,), F32)
    prm_row = jnp.zeros((SUBLANES, LANES), F32)
    prm_row = prm_row.at[0, :ng].set(jnp.concatenate([pad16, a_log]))
    prm_row = prm_row.at[1, :ng].set(jnp.concatenate([pad16, dt_bias]))
    prm_pair = jnp.repeat(jnp.stack([a_log, dt_bias]).reshape(2, GDN_QK_HEADS, 2), CHUNK, axis=2)
    a_mat, attn, rhs, qg, kt, gt = _gdn_prep(proj, gates, gates_rp, conv_w, prm_row, prm_pair, nchunk)
    tinv = _tri_inv(a_mat.reshape(b * nc * GDN_QK_HEADS * CHUNK, LANES)).reshape(a_mat.shape)
    o = _gdn_scan(rhs, qg, kt, attn, tinv, gt, proj, norm_w.reshape(1, GDN_HEAD_DIM), nchunk)
    return o.reshape(b * seq_len, GDN_VAL_DIM)


def _mlstm_kernel(q_ref, k_ref, v_ref, og_ref, gc_ref, gr_ref, brow_ref, bcol_ref, nw_ref, o_ref,
                  c_scr, n_scr, m_scr, *, nchunk):
    i = pl.program_id(0)

    @pl.when(i == 0)
    def _():
        c_scr[...] = jnp.zeros_like(c_scr)
        n_scr[...] = jnp.zeros_like(n_scr)
        m_scr[...] = jnp.zeros_like(m_scr)

    r, c = _tri_incl()
    incl = r >= c
    tri = incl.astype(F32)
    tri_t = (r <= c).astype(F32)
    dk, dv = ML_QK_DIM, ML_V_DIM
    nb = q_ref.shape[0]
    chains = [(bi, h) for bi in range(nb) for h in range(ML_HEADS)]

    def chunk_body(ci, carry):
        rs = pl.ds(pl.multiple_of(ci * CHUNK, CHUNK), CHUNK)
        t = (i * nchunk + ci) * CHUNK + lax.broadcasted_iota(jnp.int32, (CHUNK, 1), 0)
        valid = (t >= N_PAD).astype(F32)
        gate = {}
        for bi in range(nb):
            gcap = ML_GATE_CAP * jnp.tanh((gc_ref[bi, rs, :] + brow_ref[0:1, :]) / ML_GATE_CAP)
            gcap_r = ML_GATE_CAP * jnp.tanh((gr_ref[bi, ci] + bcol_ref[:, 0:1]) / ML_GATE_CAP)
            gate[bi] = (gcap, _dot_f32(tri, jax.nn.log_sigmoid(gcap)),
                        gcap_r, _dot_f32(jax.nn.log_sigmoid(gcap_r), tri_t))
        st = {ch: {} for ch in chains}
        for bi, h in chains:
            s = st[bi, h]
            s["q"] = q_ref[bi, rs, h * dk:(h + 1) * dk] * (dk ** -0.5)
            s["k"] = k_ref[bi, rs, h * dk:(h + 1) * dk] * valid
            s["qb"] = s["q"].astype(BF16)
            s["vb"] = (v_ref[bi, rs, h * dv:(h + 1) * dv] * valid).astype(BF16)
            s["qk"] = _dot_nt(s["qb"], s["k"].astype(BF16))
        for bi, h in chains:
            s = st[bi, h]
            gcap, bcum, gcap_r, bcum_r = gate[bi]
            b_i = bcum[:, ML_HEADS + h:ML_HEADS + h + 1]
            s["bl"] = b_i[CHUNK - 1:CHUNK, :]
            s["ms"] = m_scr[bi * ML_HEADS + h]
            s["a"] = s["bl"] - b_i + gcap[:, h:h + 1]
            s["inter"] = b_i + s["ms"]
            s["d"] = jnp.where(incl, b_i - bcum_r[ML_HEADS + h:ML_HEADS + h + 1, :] + gcap_r[h:h + 1, :], -jnp.inf)
        for ch in chains:
            s = st[ch]
            s["dmax"] = jnp.max(s["d"], -1, keepdims=True)
            s["amax"] = jnp.max(s["a"], 0, keepdims=True)
        for ch in chains:
            s = st[ch]
            s["m_t"] = jnp.maximum(s["dmax"], s["inter"])
            s["m_new"] = jnp.maximum(s["bl"] + s["ms"], s["amax"])
        for ch in chains:
            s = st[ch]
            s["kw"] = s["k"] * jnp.exp(s["a"] - s["m_new"])
            s["w_d"] = jnp.exp(s["d"] - s["m_t"]) * s["qk"]
        for ch in chains:
            s = st[ch]
            s["kv"] = _dot_tn(s["kw"].astype(BF16), s["vb"])
            s["wv"] = _dot(s["w_d"].astype(BF16), s["vb"])
        for bi, h in chains:
            s = st[bi, h]
            s["qc"] = _dot(s["qb"], c_scr[bi * ML_HEADS + h].astype(BF16))
            s["qn"] = jnp.sum(s["q"] * n_scr[bi * ML_HEADS + h], -1, keepdims=True)
            s["wsum"] = jnp.sum(s["w_d"], -1, keepdims=True)
        for ch in chains:
            s = st[ch]
            sc = jnp.exp(s["inter"] - s["m_t"])
            den = sc * s["qn"] + s["wsum"]
            s["hh"] = (sc * s["qc"] + s["wv"]) / jnp.maximum(jnp.abs(den), jnp.exp(-s["m_t"]))
        for bi, h in chains:
            s = st[bi, h]
            si = bi * ML_HEADS + h
            dec = jnp.exp(s["bl"] + s["ms"] - s["m_new"])
            c_scr[si] = dec * c_scr[si] + s["kv"]
            n_scr[si] = dec * n_scr[si] + jnp.sum(s["kw"], 0, keepdims=True)
            m_scr[si] = s["m_new"]
            vs = slice(h * dv, (h + 1) * dv)
            o_ref[bi, rs, vs] = (_rms(s["hh"], nw_ref[:, vs]) * jax.nn.sigmoid(og_ref[bi, rs, vs])).astype(BF16)
        return carry

    lax.fori_loop(0, nchunk, chunk_body, 0)


def _mlstm(a2d, b, seq_len, nw_in, w_in, gate_b, norm_w, nchunk):
    nc = seq_len // CHUNK
    ng = 2 * ML_HEADS
    w_main = w_in[:, :ML_MAIN].astype(BF16)
    w_gate = jnp.pad(w_in[:, ML_MAIN:], ((0, 0), (0, LANES - ng))).astype(BF16)
    proj, gates = _inproj(a2d, nw_in, w_main, w_gate, 1024)
    proj = proj.reshape(b, seq_len, ML_MAIN)
    gates = gates.reshape(b, seq_len, LANES)
    gates_r = jnp.swapaxes(gates[..., :ng].reshape(b, nc, CHUNK, ng), 2, 3)
    b_row = jnp.zeros((SUBLANES, LANES), F32).at[0, :ng].set(gate_b)
    b_col = jnp.zeros((ng, LANES), F32).at[:, 0].set(gate_b)
    rows = nchunk * CHUNK
    qkb = ML_QK_TOT
    o = pl.pallas_call(
        functools.partial(_mlstm_kernel, nchunk=nchunk),
        grid=(nc // nchunk,),
        in_specs=[
            pl.BlockSpec((b, rows, qkb), lambda i: (0, i, 0)),
            pl.BlockSpec((b, rows, qkb), lambda i: (0, i, 1)),
            pl.BlockSpec((b, rows, ML_V_TOT), lambda i: (0, i, 1)),
            pl.BlockSpec((b, rows, ML_V_TOT), lambda i: (0, i, 2)),
            pl.BlockSpec((b, rows, LANES), lambda i: (0, i, 0)),
            pl.BlockSpec((b, nchunk, ng, CHUNK), lambda i: (0, i, 0, 0)),
            pl.BlockSpec((SUBLANES, LANES), lambda i: (0, 0)),
            pl.BlockSpec((ng, LANES), lambda i: (0, 0)),
            pl.BlockSpec((1, ML_V_TOT), lambda i: (0, 0)),
        ],
        out_specs=pl.BlockSpec((b, rows, ML_V_TOT), lambda i: (0, i, 0)),
        out_shape=jax.ShapeDtypeStruct((b, seq_len, ML_V_TOT), BF16),
        scratch_shapes=[pltpu.VMEM((b * ML_HEADS, ML_QK_DIM, ML_V_DIM), F32),
                        pltpu.VMEM((b * ML_HEADS, 1, ML_QK_DIM), F32),
                        pltpu.VMEM((b * ML_HEADS, 1, 1), F32)],
        compiler_params=_params("arbitrary"),
        name="mlstm",
    )(proj, proj, proj, proj, gates, gates_r, b_row, b_col, norm_w.reshape(1, ML_V_TOT))
    return o.reshape(b * seq_len, ML_V_TOT)


def kernel(x, meta_tokens, norm_w, gdn_w_in, gdn_conv_w, gdn_a_log, gdn_dt_bias, gdn_norm_w, gdn_w_out,
           ml_w_in, ml_gate_b, ml_norm_w, ml_w_out, ffn_w_up, ffn_conv_w, ffn_conv_b, ffn_w_down):
    b, s, d = x.shape
    depth = norm_w.shape[0]
    pad = jnp.zeros((b, N_PAD, d), x.dtype)
    meta = jnp.broadcast_to(meta_tokens[None].astype(x.dtype), (b, N_META, d))
    hs = jnp.concatenate([pad, meta, x], 1)
    seq_len = hs.shape[1]
    nc = seq_len // CHUNK
    nchunk = 3 if nc % 3 == 0 else 1
    for i in range(depth):
        j = i // 2
        nw = norm_w[i].reshape(4, 1, d)
        hs2d = hs.reshape(b * seq_len, d)
        if i % 2 == 0:
            mix = _gated_deltanet(hs2d, b, seq_len, nw[0], gdn_w_in[j], gdn_conv_w[j], gdn_a_log[j],
                                  gdn_dt_bias[j], gdn_norm_w[j], nchunk)
            w_out = gdn_w_out[j]
        else:
            mix = _mlstm(hs2d, b, seq_len, nw[0], ml_w_in[j], ml_gate_b[j], ml_norm_w[j], nchunk)
            w_out = ml_w_out[j]
        hs = _outproj(mix, w_out.astype(BF16), nw[1], hs2d, seq_len).reshape(b, seq_len, d)
        hs = _ffn(hs, nw[2], ffn_w_up[i].astype(BF16), ffn_conv_w[i], ffn_conv_b[i][None],
                  ffn_w_down[i].astype(BF16), nw[3])
    return hs[:, N_PAD + N_META:]
```

```python
import functools

import jax
import jax.numpy as jnp
from jax import lax
from jax.experimental import pallas as pl
from jax.experimental.pallas import tpu as pltpu

F32 = jnp.float32
BF16 = jnp.bfloat16

CHUNK = 64
N_META = 16
N_PAD = CHUNK - N_META
RMS_EPS = 1e-6
L2_EPS = 1e-6

GDN_QK_HEADS = 8
GDN_V_HEADS = 16
GDN_HEAD_DIM = 128
GDN_CONV = 4
GDN_KEY_DIM = GDN_QK_HEADS * GDN_HEAD_DIM
GDN_VAL_DIM = GDN_V_HEADS * GDN_HEAD_DIM
GDN_QKV_DIM = 2 * GDN_KEY_DIM + GDN_VAL_DIM
GDN_MAIN = GDN_QKV_DIM + GDN_VAL_DIM

ML_HEADS = 4
ML_QK_DIM = 128
ML_V_DIM = 256
ML_GATE_CAP = 15.0
ML_QK_TOT = ML_HEADS * ML_QK_DIM
ML_V_TOT = ML_HEADS * ML_V_DIM
ML_MAIN = 2 * ML_QK_TOT + 2 * ML_V_TOT

FFN_CONV = 3
FFN_BLOCK = 256

LANES = 128
SUBLANES = 8
SOLVE_UNITS = 128
VMEM_LIMIT = 56 * 1024 * 1024


def _largest_tile(n, cap):
    d = (min(cap, n) // SUBLANES) * SUBLANES
    while d > SUBLANES and n % d:
        d -= SUBLANES
    assert d >= SUBLANES and n % d == 0, (n, cap)
    return d


def _params(*sem):
    return pltpu.CompilerParams(dimension_semantics=sem, vmem_limit_bytes=VMEM_LIMIT)


def _rms(x, w):
    return x * lax.rsqrt(jnp.mean(x * x, -1, keepdims=True) + RMS_EPS) * w


def _silu(x):
    return x * jax.nn.sigmoid(x)


def _dot(a, b):
    return jnp.dot(a, b, preferred_element_type=F32)


def _dot_nt(a, b):
    return lax.dot_general(a, b, (((1,), (1,)), ((), ())), preferred_element_type=F32)


def _dot_tn(a, b):
    return lax.dot_general(a, b, (((0,), (0,)), ((), ())), preferred_element_type=F32)


def _dot_f32(a, b):
    return jnp.dot(a, b, preferred_element_type=F32, precision=lax.Precision.HIGHEST)


def _tri_incl():
    r = lax.broadcasted_iota(jnp.int32, (CHUNK, CHUNK), 0)
    c = lax.broadcasted_iota(jnp.int32, (CHUNK, CHUNK), 1)
    return r, c


def _inproj_kernel(x_ref, nw_ref, wm_ref, wg_ref, om_ref, og_ref, a_scr):
    @pl.when(pl.program_id(1) == 0)
    def _():
        a = _rms(x_ref[...], nw_ref[...]).astype(BF16)
        a_scr[...] = a
        og_ref[...] = _dot(a, wg_ref[...])

    om_ref[...] = _dot(a_scr[...], wm_ref[...])


def _inproj(x2d, nw, w_main, w_gate, tn):
    m, d = x2d.shape
    n = w_main.shape[1]
    tm = _largest_tile(m, 1400)
    return pl.pallas_call(
        _inproj_kernel,
        grid=(m // tm, n // tn),
        in_specs=[
            pl.BlockSpec((tm, d), lambda i, j: (i, 0)),
            pl.BlockSpec((1, d), lambda i, j: (0, 0)),
            pl.BlockSpec((d, tn), lambda i, j: (0, j)),
            pl.BlockSpec((d, LANES), lambda i, j: (0, 0)),
        ],
        out_specs=[
            pl.BlockSpec((tm, tn), lambda i, j: (i, j)),
            pl.BlockSpec((tm, LANES), lambda i, j: (i, 0)),
        ],
        out_shape=[jax.ShapeDtypeStruct((m, n), F32), jax.ShapeDtypeStruct((m, LANES), F32)],
        scratch_shapes=[pltpu.VMEM((tm, d), BF16)],
        compiler_params=_params("parallel", "arbitrary"),
        name="inproj",
    )(x2d, nw, w_main, w_gate)


def _outproj_kernel(o_ref, w_ref, nw_ref, hs_ref, out_ref, *, tm, seq_len):
    y = _rms(_dot(o_ref[...], w_ref[...]), nw_ref[...])
    t = (pl.program_id(0) * tm) % seq_len + lax.broadcasted_iota(jnp.int32, (tm, 1), 0)
    out_ref[...] = hs_ref[...] + jnp.where(t >= N_PAD, y, 0.0)


def _outproj(o2d, w_out, nw, hs2d, seq_len):
    m, kin = o2d.shape
    d = hs2d.shape[1]
    tm = _largest_tile(seq_len, 1400)
    return pl.pallas_call(
        functools.partial(_outproj_kernel, tm=tm, seq_len=seq_len),
        grid=(m // tm,),
        in_specs=[
            pl.BlockSpec((tm, kin), lambda i: (i, 0)),
            pl.BlockSpec((kin, d), lambda i: (0, 0)),
            pl.BlockSpec((1, d), lambda i: (0, 0)),
            pl.BlockSpec((tm, d), lambda i: (i, 0)),
        ],
        out_specs=pl.BlockSpec((tm, d), lambda i: (i, 0)),
        out_shape=jax.ShapeDtypeStruct((m, d), F32),
        compiler_params=_params("parallel"),
        name="outproj",
    )(o2d, w_out, nw, hs2d)


def _ffn_kernel(h_ref, halo_ref, nw2_ref, wup_ref, cw_ref, cb_ref, wdn_ref, nw3_ref, out_ref,
                act_lo_scr, act_hi_scr, *, tm):
    i = pl.program_id(1)
    h = h_ref[0]
    halo = jnp.where(i == 0, 0.0, halo_ref[0])
    a = _rms(jnp.concatenate([halo, h], 0), nw2_ref[...]).astype(BF16)
    half = tm // 2
    a_lo, a_hi = a[:half + SUBLANES], a[half:]
    fdim = wdn_ref.shape[0]
    cols = [(slice(j * FFN_BLOCK, (j + 1) * FFN_BLOCK), slice(fdim + j * FFN_BLOCK, fdim + (j + 1) * FFN_BLOCK))
            for j in range(fdim // FFN_BLOCK)]

    def up(a_x, j):
        return [_dot(a_x, wup_ref[:, sl]) for sl in cols[j]]

    def conv(u, sl):
        cw = cw_ref[:, sl]
        return (cw[2:3] * u[SUBLANES:]
                + cw[1:2] * pltpu.roll(u, 1, 0)[SUBLANES:]
                + cw[0:1] * pltpu.roll(u, 2, 0)[SUBLANES:]
                + cb_ref[:, sl])

    def act(u, j):
        return (_silu(conv(u[0], cols[j][0])) * conv(u[1], cols[j][1])).astype(BF16)

    u_lo = up(a_lo, 0)
    for j in range(len(cols)):
        u_hi = up(a_hi, j)
        act_lo_scr[:, cols[j][0]] = act(u_lo, j)
        if j + 1 < len(cols):
            u_lo = up(a_lo, j + 1)
        act_hi_scr[:, cols[j][0]] = act(u_hi, j)
    f = jnp.concatenate([_dot(act_lo_scr[...], wdn_ref[...]), _dot(act_hi_scr[...], wdn_ref[...])], 0)
    t = i * tm + lax.broadcasted_iota(jnp.int32, (tm, 1), 0)
    out_ref[0] = h + jnp.where(t >= N_PAD, _rms(f, nw3_ref[...]), 0.0)


def _ffn(hs, nw2, w_up, conv_w, conv_b, w_down, nw3):
    b, seq_len, d = hs.shape
    tm = _largest_tile(seq_len, 700)
    hb = tm // SUBLANES
    assert tm % (2 * SUBLANES) == 0 and w_down.shape[0] % FFN_BLOCK == 0
    const = lambda bi, i: (0, 0)
    return pl.pallas_call(
        functools.partial(_ffn_kernel, tm=tm),
        grid=(b, seq_len // tm),
        in_specs=[
            pl.BlockSpec((1, tm, d), lambda bi, i: (bi, i, 0)),
            pl.BlockSpec((1, SUBLANES, d), lambda bi, i: (bi, jnp.maximum(i * hb - 1, 0), 0)),
            pl.BlockSpec((1, d), const),
            pl.BlockSpec(w_up.shape, const),
            pl.BlockSpec(conv_w.shape, const),
            pl.BlockSpec(conv_b.shape, const),
            pl.BlockSpec(w_down.shape, const),
            pl.BlockSpec((1, d), const),
        ],
        out_specs=pl.BlockSpec((1, tm, d), lambda bi, i: (bi, i, 0)),
        out_shape=jax.ShapeDtypeStruct(hs.shape, F32),
        scratch_shapes=[pltpu.VMEM((tm // 2, w_down.shape[0]), BF16)] * 2,
        compiler_params=_params("parallel", "parallel"),
        name="convffn",
    )(hs, hs, nw2, w_up, conv_w, conv_b, w_down, nw3)


def _gdn_gates_col(graw, prm_ref):
    beta = jax.nn.sigmoid(graw)
    g = -jnp.exp(prm_ref[0:1, :]) * jax.nn.softplus(graw + prm_ref[1:2, :])
    return beta, g


def _gdn_prep_kernel(qk_ref, v_ref, qkh_ref, vh_ref, cwqk_ref, cwv_ref, gc_ref, gr_ref, prow_ref, ppair_ref,
                     a_ref, attn_ref, rhs_ref, qg_ref, kt_ref, gt_ref, qn_ref, kn_ref, vv_ref, *, nchunk):
    i = pl.program_id(1)
    rows = nchunk * CHUNK
    t = i * rows + lax.broadcasted_iota(jnp.int32, (rows, 1), 0)
    valid = (t >= N_PAD).astype(F32)
    first = i == 0

    def conv_silu(x_ref, halo_ref, cw_ref, col):
        sl = slice(col * GDN_HEAD_DIM, (col + 1) * GDN_HEAD_DIM)
        halo = jnp.where(first, 0.0, halo_ref[0, :, sl])
        x = jnp.concatenate([halo, x_ref[0, :, sl]], 0)
        cw = cw_ref[:, sl]
        y = (cw[3:4] * x[SUBLANES:]
             + cw[2:3] * pltpu.roll(x, 1, 0)[SUBLANES:]
             + cw[1:2] * pltpu.roll(x, 2, 0)[SUBLANES:]
             + cw[0:1] * pltpu.roll(x, 3, 0)[SUBLANES:])
        return _silu(y)

    dh = GDN_HEAD_DIM
    nq = GDN_QK_HEADS
    for h in range(GDN_V_HEADS):
        sl = slice(h * dh, (h + 1) * dh)
        vv_ref[:, sl] = conv_silu(v_ref, vh_ref, cwv_ref, h) * valid
    qk_raw = [conv_silu(qk_ref, qkh_ref, cwqk_ref, col) for col in range(2 * nq)]
    qk_inv = [lax.rsqrt(jnp.sum(x * x, -1, keepdims=True) + L2_EPS) for x in qk_raw]
    for hq in range(nq):
        sl = slice(hq * dh, (hq + 1) * dh)
        qn_ref[:, sl] = qk_raw[hq] * qk_inv[hq] * (dh ** -0.5)
        kn_ref[:, sl] = qk_raw[nq + hq] * qk_inv[nq + hq] * valid

    r = lax.broadcasted_iota(jnp.int32, (CHUNK, LANES), 0)
    lane = lax.broadcasted_iota(jnp.int32, (CHUNK, LANES), 1)
    first_head = lane < CHUNK
    c = jnp.where(first_head, lane, lane - CHUNK)
    incl = r >= c
    strict = r > c
    r2 = lax.broadcasted_iota(jnp.int32, (LANES, LANES), 0)
    c2 = lax.broadcasted_iota(jnp.int32, (LANES, LANES), 1)
    cum_pair = ((r2 <= c2) & ((r2 < CHUNK) == (c2 < CHUNK))).astype(F32)
    rr, cc = _tri_incl()
    tri = (rr >= cc).astype(F32)
    nv = GDN_V_HEADS
    for ci in range(nchunk):
        rs = slice(ci * CHUNK, (ci + 1) * CHUNK)
        beta, g = _gdn_gates_col(gc_ref[0, rs, :], prow_ref)
        gcum = _dot_f32(tri, g)
        eg = jnp.exp(gcum)
        glast = gcum[CHUNK - 1:CHUNK, :]
        ktmul = jnp.exp(glast - gcum)
        egl = jnp.exp(glast)
        g_rp = -jnp.exp(ppair_ref[0]) * jax.nn.softplus(gr_ref[0, ci] + ppair_ref[1])
        gcum_rp = _dot_f32(g_rp, cum_pair)
        beg = beta * pltpu.roll(eg, LANES - nv, 1)
        qs = [qn_ref[rs, hq * dh:(hq + 1) * dh] for hq in range(nq)]
        ks = [kn_ref[rs, hq * dh:(hq + 1) * dh] for hq in range(nq)]
        k2 = [jnp.concatenate([k.astype(BF16)] * 2, 0) for k in ks]
        kk = [_dot_nt(k2[hq][:CHUNK], k2[hq]) for hq in range(nq)]
        qk = [_dot_nt(qs[hq].astype(BF16), k2[hq]) for hq in range(nq)]
        col = lambda x, j: jnp.broadcast_to(x[:, j:j + 1], (CHUNK, LANES))
        beta_b = [col(beta, h) for h in range(nv)]
        gi_b = [col(gcum, nv + h) for h in range(nv)]
        for hq in range(nq):
            h0, h1 = 2 * hq, 2 * hq + 1
            dec_p = jnp.exp(jnp.where(incl, jnp.where(first_head, gi_b[h0], gi_b[h1]) - gcum_rp[hq:hq + 1, :],
                                      -jnp.inf))
            a_ref[0, ci, hq] = jnp.where(strict, kk[hq] * jnp.where(first_head, beta_b[h0], beta_b[h1]) * dec_p, 0.0)
            attn_p = (qk[hq] * dec_p).astype(BF16)
            attn_ref[0, ci, h0] = attn_p[:, :CHUNK]
            attn_ref[0, ci, h1] = attn_p[:, CHUNK:]
        beg_b = [col(beg, h) for h in range(nv)]
        eg_b = [col(eg, nv + h) for h in range(nv)]
        kt_b = [col(ktmul, nv + h) for h in range(nv)]
        for h in range(nv):
            hs = slice(h * dh, (h + 1) * dh)
            rhs_ref[0, rs, 2 * h * dh:(2 * h + 1) * dh] = (vv_ref[rs, hs] * beta_b[h]).astype(BF16)
            rhs_ref[0, rs, (2 * h + 1) * dh:(2 * h + 2) * dh] = (ks[h // 2] * beg_b[h]).astype(BF16)
            qg_ref[0, rs, hs] = (qs[h // 2] * eg_b[h]).astype(BF16)
            kt_ref[0, rs, hs] = (ks[h // 2] * kt_b[h]).astype(BF16)
            gt_ref[0, ci, h:h + 1, :] = jnp.broadcast_to(egl[:, nv + h:nv + h + 1], (1, LANES))


def _gdn_prep(proj, gates, gates_rp, conv_w, prm_row, prm_pair, nchunk):
    b, seq_len, _ = proj.shape
    nc = seq_len // CHUNK
    rows = nchunk * CHUNK
    hb = rows // SUBLANES
    half = GDN_QKV_DIM // 2
    halo_map0 = lambda bi, i: (bi, jnp.maximum(i * hb - 1, 0), 0)
    halo_map1 = lambda bi, i: (bi, jnp.maximum(i * hb - 1, 0), 1)
    row_map = lambda bi, i: (bi, i, 0)
    return pl.pallas_call(
        functools.partial(_gdn_prep_kernel, nchunk=nchunk),
        grid=(b, nc // nchunk),
        in_specs=[
            pl.BlockSpec((1, rows, half), lambda bi, i: (bi, i, 0)),
            pl.BlockSpec((1, rows, half), lambda bi, i: (bi, i, 1)),
            pl.BlockSpec((1, SUBLANES, half), halo_map0),
            pl.BlockSpec((1, SUBLANES, half), halo_map1),
            pl.BlockSpec((GDN_CONV, half), lambda bi, i: (0, 0)),
            pl.BlockSpec((GDN_CONV, half), lambda bi, i: (0, 1)),
            pl.BlockSpec((1, rows, LANES), row_map),
            pl.BlockSpec((1, nchunk, GDN_QK_HEADS, LANES), lambda bi, i: (bi, i, 0, 0)),
            pl.BlockSpec((SUBLANES, LANES), lambda bi, i: (0, 0)),
            pl.BlockSpec((2, GDN_QK_HEADS, LANES), lambda bi, i: (0, 0, 0)),
        ],
        out_specs=[
            pl.BlockSpec((1, nchunk, GDN_QK_HEADS, CHUNK, LANES), lambda bi, i: (bi, i, 0, 0, 0)),
            pl.BlockSpec((1, nchunk, GDN_V_HEADS, CHUNK, CHUNK), lambda bi, i: (bi, i, 0, 0, 0)),
            pl.BlockSpec((1, rows, 2 * GDN_VAL_DIM), row_map),
            pl.BlockSpec((1, rows, GDN_VAL_DIM), row_map),
            pl.BlockSpec((1, rows, GDN_VAL_DIM), row_map),
            pl.BlockSpec((1, nchunk, GDN_V_HEADS, LANES), lambda bi, i: (bi, i, 0, 0)),
        ],
        out_shape=[
            jax.ShapeDtypeStruct((b, nc, GDN_QK_HEADS, CHUNK, LANES), F32),
            jax.ShapeDtypeStruct((b, nc, GDN_V_HEADS, CHUNK, CHUNK), BF16),
            jax.ShapeDtypeStruct((b, seq_len, 2 * GDN_VAL_DIM), BF16),
            jax.ShapeDtypeStruct((b, seq_len, GDN_VAL_DIM), BF16),
            jax.ShapeDtypeStruct((b, seq_len, GDN_VAL_DIM), BF16),
            jax.ShapeDtypeStruct((b, nc, GDN_V_HEADS, LANES), F32),
        ],
        scratch_shapes=[pltpu.VMEM((rows, GDN_KEY_DIM), F32), pltpu.VMEM((rows, GDN_KEY_DIM), F32),
                        pltpu.VMEM((rows, GDN_VAL_DIM), F32)],
        compiler_params=_params("parallel", "parallel"),
        name="gdn_prep",
    )(proj, proj, proj, proj, conv_w, conv_w, gates, gates_rp, prm_row, prm_pair)


def _tri_inv_kernel(a_ref, x_ref, at_scr, xt_scr):
    for i in range(CHUNK):
        at_scr[i] = a_ref[pl.ds(i, SOLVE_UNITS, stride=CHUNK), :].T

    xt_scr[...] = jnp.zeros_like(xt_scr)
    jrow = lax.broadcasted_iota(jnp.int32, (SUBLANES, SOLVE_UNITS), 0)
    sub = SUBLANES
    heads = LANES // CHUNK
    for ib in range(CHUNK // sub):
        def row_body(ii, carry, ib=ib):
            i = ib * sub + ii
            for e in range(heads):
                base = e * CHUNK
                acc = [(jrow + jb * sub == i).astype(F32) for jb in range(ib + 1)]
                for kb in range(ib + 1):
                    for kk in range(sub):
                        k = base + kb * sub + kk
                        a = at_scr[i, k:k + 1, :]
                        for jb in range(kb + 1):
                            acc[jb] = acc[jb] - a * xt_scr[kb * sub + kk, base + jb * sub:base + (jb + 1) * sub, :]
                for jb in range(ib + 1):
                    xt_scr[i, base + jb * sub:base + (jb + 1) * sub, :] = acc[jb]
            return carry

        lax.fori_loop(0, sub, row_body, 0)

    for i in range(CHUNK):
        x_ref[pl.ds(i, SOLVE_UNITS, stride=CHUNK), :] = xt_scr[i].T


def _tri_inv(a_rows):
    blk = (SOLVE_UNITS * CHUNK, LANES)
    return pl.pallas_call(
        _tri_inv_kernel,
        grid=(pl.cdiv(a_rows.shape[0], blk[0]),),
        in_specs=[pl.BlockSpec(blk, lambda i: (i, 0))],
        out_specs=pl.BlockSpec(blk, lambda i: (i, 0)),
        out_shape=jax.ShapeDtypeStruct(a_rows.shape, F32),
        scratch_shapes=[pltpu.VMEM((CHUNK, LANES, SOLVE_UNITS), F32),
                        pltpu.VMEM((CHUNK, LANES, SOLVE_UNITS), F32)],
        compiler_params=_params("parallel"),
        name="tri_inv",
    )(a_rows)


def _gdn_scan_kernel(rhs_ref, qg_ref, kt_ref, attn_ref, ti_ref, gt_ref, z_ref, nw_ref, o_ref, s_scr, *, nchunk):
    @pl.when(pl.program_id(1) == 0)
    def _():
        s_scr[...] = jnp.zeros_like(s_scr)

    rep = GDN_V_HEADS // GDN_QK_HEADS
    dh = GDN_HEAD_DIM

    def chunk_body(ci, carry):
        rs = pl.ds(pl.multiple_of(ci * CHUNK, CHUNK), CHUNK)
        pre = []
        for hq in range(GDN_QK_HEADS):
            tinv = ti_ref[0, ci, hq].astype(BF16)
            for e in range(rep):
                h = hq * rep + e
                sol = _dot(tinv[:, e * CHUNK:(e + 1) * CHUNK], rhs_ref[0, rs, 2 * h * dh:(2 * h + 2) * dh])
                u, w = sol[:, :dh], sol[:, dh:]
                pre.append((u, jnp.concatenate([w.astype(BF16), qg_ref[0, rs, h * dh:(h + 1) * dh]], 0)))
        mid = []
        for h in range(GDN_V_HEADS):
            u, wq = pre[h]
            ws_qs = _dot(wq, s_scr[h].astype(BF16))
            mid.append((ws_qs[CHUNK:], (u - ws_qs[:CHUNK]).astype(BF16)))
        for h in range(GDN_V_HEADS):
            qs, vb = mid[h]
            hs = slice(h * dh, (h + 1) * dh)
            o = qs + _dot(attn_ref[0, ci, h], vb)
            s_scr[h] = s_scr[h] * gt_ref[0, ci, h:h + 1, :] + _dot_tn(kt_ref[0, rs, hs], vb)
            o_ref[0, rs, hs] = (_rms(o, nw_ref[...]) * _silu(z_ref[0, rs, hs])).astype(BF16)
        return carry

    lax.fori_loop(0, nchunk, chunk_body, 0)


def _gdn_scan(rhs, qg, kt, attn, tinv, gt, proj, norm_w, nchunk):
    b, seq_len, _ = qg.shape
    nc = seq_len // CHUNK
    rows = nchunk * CHUNK
    zblk = GDN_QKV_DIM // GDN_VAL_DIM
    row_map = lambda bi, i: (bi, i, 0)
    return pl.pallas_call(
        functools.partial(_gdn_scan_kernel, nchunk=nchunk),
        grid=(b, nc // nchunk),
        in_specs=[
            pl.BlockSpec((1, rows, 2 * GDN_VAL_DIM), row_map),
            pl.BlockSpec((1, rows, GDN_VAL_DIM), row_map),
            pl.BlockSpec((1, rows, GDN_VAL_DIM), row_map),
            pl.BlockSpec((1, nchunk, GDN_V_HEADS, CHUNK, CHUNK), lambda bi, i: (bi, i, 0, 0, 0)),
            pl.BlockSpec((1, nchunk, GDN_QK_HEADS, CHUNK, LANES), lambda bi, i: (bi, i, 0, 0, 0)),
            pl.BlockSpec((1, nchunk, GDN_V_HEADS, LANES), lambda bi, i: (bi, i, 0, 0)),
            pl.BlockSpec((1, rows, GDN_VAL_DIM), lambda bi, i: (bi, i, zblk)),
            pl.BlockSpec((1, GDN_HEAD_DIM), lambda bi, i: (0, 0)),
        ],
        out_specs=pl.BlockSpec((1, rows, GDN_VAL_DIM), row_map),
        out_shape=jax.ShapeDtypeStruct((b, seq_len, GDN_VAL_DIM), BF16),
        scratch_shapes=[pltpu.VMEM((GDN_V_HEADS, GDN_HEAD_DIM, GDN_HEAD_DIM), F32)],
        compiler_params=_params("parallel", "arbitrary"),
        name="gdn_scan",
    )(rhs, qg, kt, attn, tinv, gt, proj, norm_w)


def _gated_deltanet(a2d, b, seq_len, nw_in, w_in, conv_w, a_log, dt_bias, norm_w, nchunk):
    nc = seq_len // CHUNK
    ng = 2 * GDN_V_HEADS
    w_main = w_in[:, :GDN_MAIN].astype(BF16)
    w_gate = jnp.pad(w_in[:, GDN_MAIN:], ((0, 0), (0, LANES - ng))).astype(BF16)
    proj, gates = _inproj(a2d, nw_in, w_main, w_gate, GDN_MAIN // 3)
    proj = proj.reshape(b, seq_len, GDN_MAIN)
    gates = gates.reshape(b, seq_len, LANES)
    assert (GDN_V_HEADS // GDN_QK_HEADS) * CHUNK == LANES
    gates_rp = jnp.swapaxes(gates[..., GDN_V_HEADS:ng].reshape(b, nc, CHUNK, GDN_V_HEADS), 2, 3)
    gates_rp = gates_rp.reshape(b, nc, GDN_QK_HEADS, LANES)
    pad16 = jnp.zeros((GDN_V_HEADS,), F32)
    prm_row = jnp.zeros((SUBLANES, LANES), F32)
    prm_row = prm_row.at[0, :ng].set(jnp.concatenate([pad16, a_log]))
    prm_row = prm_row.at[1, :ng].set(jnp.concatenate([pad16, dt_bias]))
    prm_pair = jnp.repeat(jnp.stack([a_log, dt_bias]).reshape(2, GDN_QK_HEADS, 2), CHUNK, axis=2)
    a_mat, attn, rhs, qg, kt, gt = _gdn_prep(proj, gates, gates_rp, conv_w, prm_row, prm_pair, nchunk)
    tinv = _tri_inv(a_mat.reshape(b * nc * GDN_QK_HEADS * CHUNK, LANES)).reshape(a_mat.shape)
    o = _gdn_scan(rhs, qg, kt, attn, tinv, gt, proj, norm_w.reshape(1, GDN_HEAD_DIM), nchunk)
    return o.reshape(b * seq_len, GDN_VAL_DIM)


def _mlstm_kernel(q_ref, k_ref, v_ref, og_ref, gc_ref, gr_ref, brow_ref, bcol_ref, nw_ref, o_ref,
                  c_scr, n_scr, m_scr, *, nchunk):
    i = pl.program_id(0)

    @pl.when(i == 0)
    def _():
        c_scr[...] = jnp.zeros_like(c_scr)
        n_scr[...] = jnp.zeros_like(n_scr)
        m_scr[...] = jnp.zeros_like(m_scr)

    r, c = _tri_incl()
    incl = r >= c
    tri = incl.astype(F32)
    tri_t = (r <= c).astype(F32)
    dk, dv = ML_QK_DIM, ML_V_DIM
    nb = q_ref.shape[0]
    chains = [(bi, h) for bi in range(nb) for h in range(ML_HEADS)]

    def chunk_body(ci, carry):
        rs = pl.ds(pl.multiple_of(ci * CHUNK, CHUNK), CHUNK)
        t = (i * nchunk + ci) * CHUNK + lax.broadcasted_iota(jnp.int32, (CHUNK, 1), 0)
        valid = (t >= N_PAD).astype(F32)
        gate = {}
        for bi in range(nb):
            gcap = ML_GATE_CAP * jnp.tanh((gc_ref[bi, rs, :] + brow_ref[0:1, :]) / ML_GATE_CAP)
            gcap_r = ML_GATE_CAP * jnp.tanh((gr_ref[bi, ci] + bcol_ref[:, 0:1]) / ML_GATE_CAP)
            gate[bi] = (gcap, _dot_f32(tri, jax.nn.log_sigmoid(gcap)),
                        gcap_r, _dot_f32(jax.nn.log_sigmoid(gcap_r), tri_t))
        st = {ch: {} for ch in chains}
        for bi, h in chains:
            s = st[bi, h]
            s["q"] = q_ref[bi, rs, h * dk:(h + 1) * dk] * (dk ** -0.5)
            s["k"] = k_ref[bi, rs, h * dk:(h + 1) * dk] * valid
            s["qb"] = s["q"].astype(BF16)
            s["vb"] = (v_ref[bi, rs, h * dv:(h + 1) * dv] * valid).astype(BF16)
            s["qk"] = _dot_nt(s["qb"], s["k"].astype(BF16))
        for bi, h in chains:
            s = st[bi, h]
            gcap, bcum, gcap_r, bcum_r = gate[bi]
            b_i = bcum[:, ML_HEADS + h:ML_HEADS + h + 1]
            s["bl"] = b_i[CHUNK - 1:CHUNK, :]
            s["ms"] = m_scr[bi * ML_HEADS + h]
            s["a"] = s["bl"] - b_i + gcap[:, h:h + 1]
            s["inter"] = b_i + s["ms"]
            s["d"] = jnp.where(incl, b_i - bcum_r[ML_HEADS + h:ML_HEADS + h + 1, :] + gcap_r[h:h + 1, :], -jnp.inf)
        for ch in chains:
            s = st[ch]
            s["dmax"] = jnp.max(s["d"], -1, keepdims=True)
            s["amax"] = jnp.max(s["a"], 0, keepdims=True)
        for ch in chains:
            s = st[ch]
            s["m_t"] = jnp.maximum(s["dmax"], s["inter"])
            s["m_new"] = jnp.maximum(s["bl"] + s["ms"], s["amax"])
        for ch in chains:
            s = st[ch]
            s["kw"] = s["k"] * jnp.exp(s["a"] - s["m_new"])
            s["w_d"] = jnp.exp(s["d"] - s["m_t"]) * s["qk"]
        for ch in chains:
            s = st[ch]
            s["kv"] = _dot_tn(s["kw"].astype(BF16), s["vb"])
            s["wv"] = _dot(s["w_d"].astype(BF16), s["vb"])
        for bi, h in chains:
            s = st[bi, h]
            s["qc"] = _dot(s["qb"], c_scr[bi * ML_HEADS + h].astype(BF16))
            s["qn"] = jnp.sum(s["q"] * n_scr[bi * ML_HEADS + h], -1, keepdims=True)
            s["wsum"] = jnp.sum(s["w_d"], -1, keepdims=True)
        for ch in chains:
            s = st[ch]
            sc = jnp.exp(s["inter"] - s["m_t"])
            den = sc * s["qn"] + s["wsum"]
            s["hh"] = (sc * s["qc"] + s["wv"]) / jnp.maximum(jnp.abs(den), jnp.exp(-s["m_t"]))
        for bi, h in chains:
            s = st[bi, h]
            si = bi * ML_HEADS + h
            dec = jnp.exp(s["bl"] + s["ms"] - s["m_new"])
            c_scr[si] = dec * c_scr[si] + s["kv"]
            n_scr[si] = dec * n_scr[si] + jnp.sum(s["kw"], 0, keepdims=True)
            m_scr[si] = s["m_new"]
            vs = slice(h * dv, (h + 1) * dv)
            o_ref[bi, rs, vs] = (_rms(s["hh"], nw_ref[:, vs]) * jax.nn.sigmoid(og_ref[bi, rs, vs])).astype(BF16)
        return carry

    lax.fori_loop(0, nchunk, chunk_body, 0)


def _mlstm(a2d, b, seq_len, nw_in, w_in, gate_b, norm_w, nchunk):
    nc = seq_len // CHUNK
    ng = 2 * ML_HEADS
    w_main = w_in[:, :ML_MAIN].astype(BF16)
    w_gate = jnp.pad(w_in[:, ML_MAIN:], ((0, 0), (0, LANES - ng))).astype(BF16)
    proj, gates = _inproj(a2d, nw_in, w_main, w_gate, ML_MAIN // 2)
    proj = proj.reshape(b, seq_len, ML_MAIN)
    gates = gates.reshape(b, seq_len, LANES)
    gates_r = jnp.swapaxes(gates[..., :ng].reshape(b, nc, CHUNK, ng), 2, 3)
    b_row = jnp.zeros((SUBLANES, LANES), F32).at[0, :ng].set(gate_b)
    b_col = jnp.zeros((ng, LANES), F32).at[:, 0].set(gate_b)
    rows = nchunk * CHUNK
    qkb = ML_QK_TOT
    o = pl.pallas_call(
        functools.partial(_mlstm_kernel, nchunk=nchunk),
        grid=(nc // nchunk,),
        in_specs=[
            pl.BlockSpec((b, rows, qkb), lambda i: (0, i, 0)),
            pl.BlockSpec((b, rows, qkb), lambda i: (0, i, 1)),
            pl.BlockSpec((b, rows, ML_V_TOT), lambda i: (0, i, 1)),
            pl.BlockSpec((b, rows, ML_V_TOT), lambda i: (0, i, 2)),
            pl.BlockSpec((b, rows, LANES), lambda i: (0, i, 0)),
            pl.BlockSpec((b, nchunk, ng, CHUNK), lambda i: (0, i, 0, 0)),
            pl.BlockSpec((SUBLANES, LANES), lambda i: (0, 0)),
            pl.BlockSpec((ng, LANES), lambda i: (0, 0)),
            pl.BlockSpec((1, ML_V_TOT), lambda i: (0, 0)),
        ],
        out_specs=pl.BlockSpec((b, rows, ML_V_TOT), lambda i: (0, i, 0)),
        out_shape=jax.ShapeDtypeStruct((b, seq_len, ML_V_TOT), BF16),
        scratch_shapes=[pltpu.VMEM((b * ML_HEADS, ML_QK_DIM, ML_V_DIM), F32),
                        pltpu.VMEM((b * ML_HEADS, 1, ML_QK_DIM), F32),
                        pltpu.VMEM((b * ML_HEADS, 1, 1), F32)],
        compiler_params=_params("arbitrary"),
        name="mlstm",
    )(proj, proj, proj, proj, gates, gates_r, b_row, b_col, norm_w.reshape(1, ML_V_TOT))
    return o.reshape(b * seq_len, ML_V_TOT)


def kernel(x, meta_tokens, norm_w, gdn_w_in, gdn_conv_w, gdn_a_log, gdn_dt_bias, gdn_norm_w, gdn_w_out,
           ml_w_in, ml_gate_b, ml_norm_w, ml_w_out, ffn_w_up, ffn_conv_w, ffn_conv_b, ffn_w_down):
    b, s, d = x.shape
    depth = norm_w.shape[0]
    pad = jnp.zeros((b, N_PAD, d), x.dtype)
    meta = jnp.broadcast_to(meta_tokens[None].astype(x.dtype), (b, N_META, d))
    hs = jnp.concatenate([pad, meta, x], 1)
    seq_len = hs.shape[1]
    nc = seq_len // CHUNK
    nchunk = 3 if nc % 3 == 0 else 1
    for i in range(depth):
        j = i // 2
        nw = norm_w[i].reshape(4, 1, d)
        hs2d = hs.reshape(b * seq_len, d)
        if i % 2 == 0:
            mix = _gated_deltanet(hs2d, b, seq_len, nw[0], gdn_w_in[j], gdn_conv_w[j], gdn_a_log[j],
                                  gdn_dt_bias[j], gdn_norm_w[j], nchunk)
            w_out = gdn_w_out[j]
        else:
            mix = _mlstm(hs2d, b, seq_len, nw[0], ml_w_in[j], ml_gate_b[j], ml_norm_w[j], nchunk)
            w_out = ml_w_out[j]
        hs = _outproj(mix, w_out.astype(BF16), nw[1], hs2d, seq_len).reshape(b, seq_len, d)
        hs = _ffn(hs, nw[2], ffn_w_up[i].astype(BF16), ffn_conv_w[i], ffn_conv_b[i][None],
                  ffn_w_down[i].astype(BF16), nw[3])
    return hs[:, N_PAD + N_META:]
```

```python
import functools

import jax
import jax.numpy as jnp
from jax import lax
from jax.experimental import pallas as pl
from jax.experimental.pallas import tpu as pltpu

F32 = jnp.float32
BF16 = jnp.bfloat16

CHUNK = 64
N_META = 16
N_PAD = CHUNK - N_META
RMS_EPS = 1e-6
L2_EPS = 1e-6

GDN_QK_HEADS = 8
GDN_V_HEADS = 16
GDN_HEAD_DIM = 128
GDN_CONV = 4
GDN_KEY_DIM = GDN_QK_HEADS * GDN_HEAD_DIM
GDN_VAL_DIM = GDN_V_HEADS * GDN_HEAD_DIM
GDN_QKV_DIM = 2 * GDN_KEY_DIM + GDN_VAL_DIM
GDN_MAIN = GDN_QKV_DIM + GDN_VAL_DIM

ML_HEADS = 4
ML_QK_DIM = 128
ML_V_DIM = 256
ML_GATE_CAP = 15.0
ML_QK_TOT = ML_HEADS * ML_QK_DIM
ML_V_TOT = ML_HEADS * ML_V_DIM
ML_MAIN = 2 * ML_QK_TOT + 2 * ML_V_TOT

FFN_CONV = 3
FFN_BLOCK = 256

LANES = 128
SUBLANES = 8
SOLVE_UNITS = 128
VMEM_LIMIT = 56 * 1024 * 1024


def _largest_tile(n, cap):
    d = (min(cap, n) // SUBLANES) * SUBLANES
    while d > SUBLANES and n % d:
        d -= SUBLANES
    assert d >= SUBLANES and n % d == 0, (n, cap)
    return d


def _params(*sem):
    return pltpu.CompilerParams(dimension_semantics=sem, vmem_limit_bytes=VMEM_LIMIT)


def _rms(x, w):
    return x * lax.rsqrt(jnp.mean(x * x, -1, keepdims=True) + RMS_EPS) * w


def _silu(x):
    return x * jax.nn.sigmoid(x)


def _dot(a, b):
    return jnp.dot(a, b, preferred_element_type=F32)


def _dot_nt(a, b):
    return lax.dot_general(a, b, (((1,), (1,)), ((), ())), preferred_element_type=F32)


def _dot_tn(a, b):
    return lax.dot_general(a, b, (((0,), (0,)), ((), ())), preferred_element_type=F32)


def _dot_f32(a, b):
    return jnp.dot(a, b, preferred_element_type=F32, precision=lax.Precision.HIGHEST)


def _tri_incl():
    r = lax.broadcasted_iota(jnp.int32, (CHUNK, CHUNK), 0)
    c = lax.broadcasted_iota(jnp.int32, (CHUNK, CHUNK), 1)
    return r, c


def _inproj_kernel(x_ref, nw_ref, wm_ref, wg_ref, om_ref, og_ref, a_scr):
    @pl.when(pl.program_id(1) == 0)
    def _():
        a = _rms(x_ref[...], nw_ref[...]).astype(BF16)
        a_scr[...] = a
        og_ref[...] = _dot(a, wg_ref[...])

    om_ref[...] = _dot(a_scr[...], wm_ref[...])


def _inproj(x2d, nw, w_main, w_gate, tn):
    m, d = x2d.shape
    n = w_main.shape[1]
    tm = _largest_tile(m, 1400)
    return pl.pallas_call(
        _inproj_kernel,
        grid=(m // tm, n // tn),
        in_specs=[
            pl.BlockSpec((tm, d), lambda i, j: (i, 0)),
            pl.BlockSpec((1, d), lambda i, j: (0, 0)),
            pl.BlockSpec((d, tn), lambda i, j: (0, j)),
            pl.BlockSpec((d, LANES), lambda i, j: (0, 0)),
        ],
        out_specs=[
            pl.BlockSpec((tm, tn), lambda i, j: (i, j)),
            pl.BlockSpec((tm, LANES), lambda i, j: (i, 0)),
        ],
        out_shape=[jax.ShapeDtypeStruct((m, n), F32), jax.ShapeDtypeStruct((m, LANES), F32)],
        scratch_shapes=[pltpu.VMEM((tm, d), BF16)],
        compiler_params=_params("parallel", "arbitrary"),
        name="inproj",
    )(x2d, nw, w_main, w_gate)


def _outproj_kernel(o_ref, w_ref, nw_ref, hs_ref, out_ref, *, tm, seq_len):
    y = _rms(_dot(o_ref[...], w_ref[...]), nw_ref[...])
    t = (pl.program_id(0) * tm) % seq_len + lax.broadcasted_iota(jnp.int32, (tm, 1), 0)
    out_ref[...] = hs_ref[...] + jnp.where(t >= N_PAD, y, 0.0)


def _outproj(o2d, w_out, nw, hs2d, seq_len):
    m, kin = o2d.shape
    d = hs2d.shape[1]
    tm = _largest_tile(seq_len, 1400)
    return pl.pallas_call(
        functools.partial(_outproj_kernel, tm=tm, seq_len=seq_len),
        grid=(m // tm,),
        in_specs=[
            pl.BlockSpec((tm, kin), lambda i: (i, 0)),
            pl.BlockSpec((kin, d), lambda i: (0, 0)),
            pl.BlockSpec((1, d), lambda i: (0, 0)),
            pl.BlockSpec((tm, d), lambda i: (i, 0)),
        ],
        out_specs=pl.BlockSpec((tm, d), lambda i: (i, 0)),
        out_shape=jax.ShapeDtypeStruct((m, d), F32),
        compiler_params=_params("parallel"),
        name="outproj",
    )(o2d, w_out, nw, hs2d)


def _ffn_kernel(h_ref, halo_ref, nw2_ref, wup_ref, cw_ref, cb_ref, wdn_ref, nw3_ref, out_ref,
                act_lo_scr, act_hi_scr, *, tm):
    i = pl.program_id(1)
    h = h_ref[0]
    halo = jnp.where(i == 0, 0.0, halo_ref[0])
    a = _rms(jnp.concatenate([halo, h], 0), nw2_ref[...]).astype(BF16)
    half = tm // 2
    a_lo, a_hi = a[:half + SUBLANES], a[half:]
    fdim = wdn_ref.shape[0]
    cols = [(slice(j * FFN_BLOCK, (j + 1) * FFN_BLOCK), slice(fdim + j * FFN_BLOCK, fdim + (j + 1) * FFN_BLOCK))
            for j in range(fdim // FFN_BLOCK)]

    def up(a_x, j):
        return [_dot(a_x, wup_ref[:, sl]) for sl in cols[j]]

    def conv(u, sl):
        cw = cw_ref[:, sl]
        return (cw[2:3] * u[SUBLANES:]
                + cw[1:2] * pltpu.roll(u, 1, 0)[SUBLANES:]
                + cw[0:1] * pltpu.roll(u, 2, 0)[SUBLANES:]
                + cb_ref[:, sl])

    def act(u, j):
        return (_silu(conv(u[0], cols[j][0])) * conv(u[1], cols[j][1])).astype(BF16)

    u_lo = up(a_lo, 0)
    for j in range(len(cols)):
        u_hi = up(a_hi, j)
        act_lo_scr[:, cols[j][0]] = act(u_lo, j)
        if j + 1 < len(cols):
            u_lo = up(a_lo, j + 1)
        act_hi_scr[:, cols[j][0]] = act(u_hi, j)
    f = jnp.concatenate([_dot(act_lo_scr[...], wdn_ref[...]), _dot(act_hi_scr[...], wdn_ref[...])], 0)
    t = i * tm + lax.broadcasted_iota(jnp.int32, (tm, 1), 0)
    out_ref[0] = h + jnp.where(t >= N_PAD, _rms(f, nw3_ref[...]), 0.0)


def _ffn(hs, nw2, w_up, conv_w, conv_b, w_down, nw3):
    b, seq_len, d = hs.shape
    tm = _largest_tile(seq_len, 700)
    hb = tm // SUBLANES
    assert tm % (2 * SUBLANES) == 0 and w_down.shape[0] % FFN_BLOCK == 0
    const = lambda bi, i: (0, 0)
    return pl.pallas_call(
        functools.partial(_ffn_kernel, tm=tm),
        grid=(b, seq_len // tm),
        in_specs=[
            pl.BlockSpec((1, tm, d), lambda bi, i: (bi, i, 0)),
            pl.BlockSpec((1, SUBLANES, d), lambda bi, i: (bi, jnp.maximum(i * hb - 1, 0), 0)),
            pl.BlockSpec((1, d), const),
            pl.BlockSpec(w_up.shape, const),
            pl.BlockSpec(conv_w.shape, const),
            pl.BlockSpec(conv_b.shape, const),
            pl.BlockSpec(w_down.shape, const),
            pl.BlockSpec((1, d), const),
        ],
        out_specs=pl.BlockSpec((1, tm, d), lambda bi, i: (bi, i, 0)),
        out_shape=jax.ShapeDtypeStruct(hs.shape, F32),
        scratch_shapes=[pltpu.VMEM((tm // 2, w_down.shape[0]), BF16)] * 2,
        compiler_params=_params("parallel", "parallel"),
        name="convffn",
    )(hs, hs, nw2, w_up, conv_w, conv_b, w_down, nw3)


def _gdn_gates_col(graw, prm_ref):
    beta = jax.nn.sigmoid(graw)
    g = -jnp.exp(prm_ref[0:1, :]) * jax.nn.softplus(graw + prm_ref[1:2, :])
    return beta, g


def _gdn_prep_kernel(qk_ref, v_ref, qkh_ref, vh_ref, cwqk_ref, cwv_ref, gc_ref, gr_ref, prow_ref, ppair_ref,
                     a_ref, attn_ref, rhs_ref, qg_ref, kt_ref, gt_ref, qn_ref, kn_ref, vv_ref, *, nchunk):
    i = pl.program_id(1)
    rows = nchunk * CHUNK
    t = i * rows + lax.broadcasted_iota(jnp.int32, (rows, 1), 0)
    valid = (t >= N_PAD).astype(F32)
    first = i == 0

    def conv_silu(x_ref, halo_ref, cw_ref, col):
        sl = slice(col * GDN_HEAD_DIM, (col + 1) * GDN_HEAD_DIM)
        halo = jnp.where(first, 0.0, halo_ref[0, :, sl])
        x = jnp.concatenate([halo, x_ref[0, :, sl]], 0)
        cw = cw_ref[:, sl]
        y = (cw[3:4] * x[SUBLANES:]
             + cw[2:3] * pltpu.roll(x, 1, 0)[SUBLANES:]
             + cw[1:2] * pltpu.roll(x, 2, 0)[SUBLANES:]
             + cw[0:1] * pltpu.roll(x, 3, 0)[SUBLANES:])
        return _silu(y)

    dh = GDN_HEAD_DIM
    nq = GDN_QK_HEADS
    for h in range(GDN_V_HEADS):
        sl = slice(h * dh, (h + 1) * dh)
        vv_ref[:, sl] = conv_silu(v_ref, vh_ref, cwv_ref, h) * valid
    qk_raw = [conv_silu(qk_ref, qkh_ref, cwqk_ref, col) for col in range(2 * nq)]
    qk_inv = [lax.rsqrt(jnp.sum(x * x, -1, keepdims=True) + L2_EPS) for x in qk_raw]
    for hq in range(nq):
        sl = slice(hq * dh, (hq + 1) * dh)
        qn_ref[:, sl] = qk_raw[hq] * qk_inv[hq] * (dh ** -0.5)
        kn_ref[:, sl] = qk_raw[nq + hq] * qk_inv[nq + hq] * valid

    r = lax.broadcasted_iota(jnp.int32, (CHUNK, LANES), 0)
    lane = lax.broadcasted_iota(jnp.int32, (CHUNK, LANES), 1)
    first_head = lane < CHUNK
    c = jnp.where(first_head, lane, lane - CHUNK)
    incl = r >= c
    strict = r > c
    r2 = lax.broadcasted_iota(jnp.int32, (LANES, LANES), 0)
    c2 = lax.broadcasted_iota(jnp.int32, (LANES, LANES), 1)
    cum_pair = ((r2 <= c2) & ((r2 < CHUNK) == (c2 < CHUNK))).astype(F32)
    rr, cc = _tri_incl()
    tri = (rr >= cc).astype(F32)
    nv = GDN_V_HEADS
    for ci in range(nchunk):
        rs = slice(ci * CHUNK, (ci + 1) * CHUNK)
        beta, g = _gdn_gates_col(gc_ref[0, rs, :], prow_ref)
        gcum = _dot_f32(tri, g)
        eg = jnp.exp(gcum)
        glast = gcum[CHUNK - 1:CHUNK, :]
        ktmul = jnp.exp(glast - gcum)
        egl = jnp.exp(glast)
        g_rp = -jnp.exp(ppair_ref[0]) * jax.nn.softplus(gr_ref[0, ci] + ppair_ref[1])
        gcum_rp = _dot_f32(g_rp, cum_pair)
        beg = beta * pltpu.roll(eg, LANES - nv, 1)
        qs = [qn_ref[rs, hq * dh:(hq + 1) * dh] for hq in range(nq)]
        ks = [kn_ref[rs, hq * dh:(hq + 1) * dh] for hq in range(nq)]
        k2 = [jnp.concatenate([k.astype(BF16)] * 2, 0) for k in ks]
        kk = [_dot_nt(k2[hq][:CHUNK], k2[hq]) for hq in range(nq)]
        qk = [_dot_nt(qs[hq].astype(BF16), k2[hq]) for hq in range(nq)]
        col = lambda x, j: jnp.broadcast_to(x[:, j:j + 1], (CHUNK, LANES))
        beta_b = [col(beta, h) for h in range(nv)]
        gi_b = [col(gcum, nv + h) for h in range(nv)]
        for hq in range(nq):
            h0, h1 = 2 * hq, 2 * hq + 1
            dec_p = jnp.exp(jnp.where(incl, jnp.where(first_head, gi_b[h0], gi_b[h1]) - gcum_rp[hq:hq + 1, :],
                                      -jnp.inf))
            a_ref[0, ci, hq] = jnp.where(strict, kk[hq] * jnp.where(first_head, beta_b[h0], beta_b[h1]) * dec_p, 0.0)
            attn_p = (qk[hq] * dec_p).astype(BF16)
            attn_ref[0, ci, h0] = attn_p[:, :CHUNK]
            attn_ref[0, ci, h1] = attn_p[:, CHUNK:]
        beg_b = [col(beg, h) for h in range(nv)]
        eg_b = [col(eg, nv + h) for h in range(nv)]
        kt_b = [col(ktmul, nv + h) for h in range(nv)]
        for h in range(nv):
            hs = slice(h * dh, (h + 1) * dh)
            rhs_ref[0, rs, 2 * h * dh:(2 * h + 1) * dh] = (vv_ref[rs, hs] * beta_b[h]).astype(BF16)
            rhs_ref[0, rs, (2 * h + 1) * dh:(2 * h + 2) * dh] = (ks[h // 2] * beg_b[h]).astype(BF16)
            qg_ref[0, rs, hs] = (qs[h // 2] * eg_b[h]).astype(BF16)
            kt_ref[0, rs, hs] = (ks[h // 2] * kt_b[h]).astype(BF16)
            gt_ref[0, ci, h:h + 1, :] = jnp.broadcast_to(egl[:, nv + h:nv + h + 1], (1, LANES))


def _gdn_prep(proj, gates, gates_rp, conv_w, prm_row, prm_pair, nchunk):
    b, seq_len, _ = proj.shape
    nc = seq_len // CHUNK
    rows = nchunk * CHUNK
    hb = rows // SUBLANES
    half = GDN_QKV_DIM // 2
    halo_map0 = lambda bi, i: (bi, jnp.maximum(i * hb - 1, 0), 0)
    halo_map1 = lambda bi, i: (bi, jnp.maximum(i * hb - 1, 0), 1)
    row_map = lambda bi, i: (bi, i, 0)
    return pl.pallas_call(
        functools.partial(_gdn_prep_kernel, nchunk=nchunk),
        grid=(b, nc // nchunk),
        in_specs=[
            pl.BlockSpec((1, rows, half), lambda bi, i: (bi, i, 0)),
            pl.BlockSpec((1, rows, half), lambda bi, i: (bi, i, 1)),
            pl.BlockSpec((1, SUBLANES, half), halo_map0),
            pl.BlockSpec((1, SUBLANES, half), halo_map1),
            pl.BlockSpec((GDN_CONV, half), lambda bi, i: (0, 0)),
            pl.BlockSpec((GDN_CONV, half), lambda bi, i: (0, 1)),
            pl.BlockSpec((1, rows, LANES), row_map),
            pl.BlockSpec((1, nchunk, GDN_QK_HEADS, LANES), lambda bi, i: (bi, i, 0, 0)),
            pl.BlockSpec((SUBLANES, LANES), lambda bi, i: (0, 0)),
            pl.BlockSpec((2, GDN_QK_HEADS, LANES), lambda bi, i: (0, 0, 0)),
        ],
        out_specs=[
            pl.BlockSpec((1, nchunk, GDN_QK_HEADS, CHUNK, LANES), lambda bi, i: (bi, i, 0, 0, 0)),
            pl.BlockSpec((1, nchunk, GDN_V_HEADS, CHUNK, CHUNK), lambda bi, i: (bi, i, 0, 0, 0)),
            pl.BlockSpec((1, rows, 2 * GDN_VAL_DIM), row_map),
            pl.BlockSpec((1, rows, GDN_VAL_DIM), row_map),
            pl.BlockSpec((1, rows, GDN_VAL_DIM), row_map),
            pl.BlockSpec((1, nchunk, GDN_V_HEADS, LANES), lambda bi, i: (bi, i, 0, 0)),
        ],
        out_shape=[
            jax.ShapeDtypeStruct((b, nc, GDN_QK_HEADS, CHUNK, LANES), F32),
            jax.ShapeDtypeStruct((b, nc, GDN_V_HEADS, CHUNK, CHUNK), BF16),
            jax.ShapeDtypeStruct((b, seq_len, 2 * GDN_VAL_DIM), BF16),
            jax.ShapeDtypeStruct((b, seq_len, GDN_VAL_DIM), BF16),
            jax.ShapeDtypeStruct((b, seq_len, GDN_VAL_DIM), BF16),
            jax.ShapeDtypeStruct((b, nc, GDN_V_HEADS, LANES), F32),
        ],
        scratch_shapes=[pltpu.VMEM((rows, GDN_KEY_DIM), F32), pltpu.VMEM((rows, GDN_KEY_DIM), F32),
                        pltpu.VMEM((rows, GDN_VAL_DIM), F32)],
        compiler_params=_params("parallel", "parallel"),
        name="gdn_prep",
    )(proj, proj, proj, proj, conv_w, conv_w, gates, gates_rp, prm_row, prm_pair)


def _tri_inv_kernel(a_ref, x_ref, at_scr, xt_scr):
    for i in range(CHUNK):
        at_scr[i] = a_ref[pl.ds(i, SOLVE_UNITS, stride=CHUNK), :].T

    xt_scr[...] = jnp.zeros_like(xt_scr)
    jrow = lax.broadcasted_iota(jnp.int32, (SUBLANES, SOLVE_UNITS), 0)
    sub = SUBLANES
    heads = LANES // CHUNK
    for ib in range(CHUNK // sub):
        def row_body(ii, carry, ib=ib):
            i = ib * sub + ii
            for e in range(heads):
                base = e * CHUNK
                acc = [(jrow + jb * sub == i).astype(F32) for jb in range(ib + 1)]
                for kb in range(ib + 1):
                    for kk in range(sub):
                        k = base + kb * sub + kk
                        a = at_scr[i, k:k + 1, :]
                        for jb in range(kb + 1):
                            acc[jb] = acc[jb] - a * xt_scr[kb * sub + kk, base + jb * sub:base + (jb + 1) * sub, :]
                for jb in range(ib + 1):
                    xt_scr[i, base + jb * sub:base + (jb + 1) * sub, :] = acc[jb]
            return carry

        lax.fori_loop(0, sub, row_body, 0)

    for i in range(CHUNK):
        x_ref[pl.ds(i, SOLVE_UNITS, stride=CHUNK), :] = xt_scr[i].T


def _tri_inv(a_rows):
    blk = (SOLVE_UNITS * CHUNK, LANES)
    return pl.pallas_call(
        _tri_inv_kernel,
        grid=(pl.cdiv(a_rows.shape[0], blk[0]),),
        in_specs=[pl.BlockSpec(blk, lambda i: (i, 0))],
        out_specs=pl.BlockSpec(blk, lambda i: (i, 0)),
        out_shape=jax.ShapeDtypeStruct(a_rows.shape, F32),
        scratch_shapes=[pltpu.VMEM((CHUNK, LANES, SOLVE_UNITS), F32),
                        pltpu.VMEM((CHUNK, LANES, SOLVE_UNITS), F32)],
        compiler_params=_params("parallel"),
        name="tri_inv",
    )(a_rows)


def _gdn_scan_kernel(rhs_ref, qg_ref, kt_ref, attn_ref, ti_ref, gt_ref, z_ref, nw_ref, o_ref, s_scr, *, nchunk):
    @pl.when(pl.program_id(0) == 0)
    def _():
        s_scr[...] = jnp.zeros_like(s_scr)

    rep = GDN_V_HEADS // GDN_QK_HEADS
    dh = GDN_HEAD_DIM
    chains = [(bi, h) for bi in range(qg_ref.shape[0]) for h in range(GDN_V_HEADS)]

    def chunk_body(ci, carry):
        rs = pl.ds(pl.multiple_of(ci * CHUNK, CHUNK), CHUNK)
        pre = {}
        for bi, h in chains:
            tinv = ti_ref[bi, ci, h // rep].astype(BF16)
            e = h % rep
            sol = _dot(tinv[:, e * CHUNK:(e + 1) * CHUNK], rhs_ref[bi, rs, 2 * h * dh:(2 * h + 2) * dh])
            u, w = sol[:, :dh], sol[:, dh:]
            pre[bi, h] = (u, jnp.concatenate([w.astype(BF16), qg_ref[bi, rs, h * dh:(h + 1) * dh]], 0))
        mid = {}
        for bi, h in chains:
            u, wq = pre[bi, h]
            ws_qs = _dot(wq, s_scr[bi * GDN_V_HEADS + h].astype(BF16))
            mid[bi, h] = (ws_qs[CHUNK:], (u - ws_qs[:CHUNK]).astype(BF16))
        for bi, h in chains:
            qs, vb = mid[bi, h]
            hs = slice(h * dh, (h + 1) * dh)
            si = bi * GDN_V_HEADS + h
            o = qs + _dot(attn_ref[bi, ci, h], vb)
            s_scr[si] = s_scr[si] * gt_ref[bi, ci, h:h + 1, :] + _dot_tn(kt_ref[bi, rs, hs], vb)
            o_ref[bi, rs, hs] = (_rms(o, nw_ref[...]) * _silu(z_ref[bi, rs, hs])).astype(BF16)
        return carry

    lax.fori_loop(0, nchunk, chunk_body, 0, unroll=True)


def _gdn_scan(rhs, qg, kt, attn, tinv, gt, proj, norm_w, nchunk):
    b, seq_len, _ = qg.shape
    nc = seq_len // CHUNK
    rows = nchunk * CHUNK
    zblk = GDN_QKV_DIM // GDN_VAL_DIM
    row_map = lambda i: (0, i, 0)
    return pl.pallas_call(
        functools.partial(_gdn_scan_kernel, nchunk=nchunk),
        grid=(nc // nchunk,),
        in_specs=[
            pl.BlockSpec((b, rows, 2 * GDN_VAL_DIM), row_map),
            pl.BlockSpec((b, rows, GDN_VAL_DIM), row_map),
            pl.BlockSpec((b, rows, GDN_VAL_DIM), row_map),
            pl.BlockSpec((b, nchunk, GDN_V_HEADS, CHUNK, CHUNK), lambda i: (0, i, 0, 0, 0)),
            pl.BlockSpec((b, nchunk, GDN_QK_HEADS, CHUNK, LANES), lambda i: (0, i, 0, 0, 0)),
            pl.BlockSpec((b, nchunk, GDN_V_HEADS, LANES), lambda i: (0, i, 0, 0)),
            pl.BlockSpec((b, rows, GDN_VAL_DIM), lambda i: (0, i, zblk)),
            pl.BlockSpec((1, GDN_HEAD_DIM), lambda i: (0, 0)),
        ],
        out_specs=pl.BlockSpec((b, rows, GDN_VAL_DIM), row_map),
        out_shape=jax.ShapeDtypeStruct((b, seq_len, GDN_VAL_DIM), BF16),
        scratch_shapes=[pltpu.VMEM((b * GDN_V_HEADS, GDN_HEAD_DIM, GDN_HEAD_DIM), F32)],
        compiler_params=_params("arbitrary"),
        name="gdn_scan",
    )(rhs, qg, kt, attn, tinv, gt, proj, norm_w)


def _gated_deltanet(a2d, b, seq_len, nw_in, w_in, conv_w, a_log, dt_bias, norm_w, nchunk):
    nc = seq_len // CHUNK
    ng = 2 * GDN_V_HEADS
    w_main = w_in[:, :GDN_MAIN].astype(BF16)
    w_gate = jnp.pad(w_in[:, GDN_MAIN:], ((0, 0), (0, LANES - ng))).astype(BF16)
    proj, gates = _inproj(a2d, nw_in, w_main, w_gate, GDN_MAIN // 3)
    proj = proj.reshape(b, seq_len, GDN_MAIN)
    gates = gates.reshape(b, seq_len, LANES)
    assert (GDN_V_HEADS // GDN_QK_HEADS) * CHUNK == LANES
    gates_rp = jnp.swapaxes(gates[..., GDN_V_HEADS:ng].reshape(b, nc, CHUNK, GDN_V_HEADS), 2, 3)
    gates_rp = gates_rp.reshape(b, nc, GDN_QK_HEADS, LANES)
    pad16 = jnp.zeros((GDN_V_HEADS,), F32)
    prm_row = jnp.zeros((SUBLANES, LANES), F32)
    prm_row = prm_row.at[0, :ng].set(jnp.concatenate([pad16, a_log]))
    prm_row = prm_row.at[1, :ng].set(jnp.concatenate([pad16, dt_bias]))
    prm_pair = jnp.repeat(jnp.stack([a_log, dt_bias]).reshape(2, GDN_QK_HEADS, 2), CHUNK, axis=2)
    a_mat, attn, rhs, qg, kt, gt = _gdn_prep(proj, gates, gates_rp, conv_w, prm_row, prm_pair, nchunk)
    tinv = _tri_inv(a_mat.reshape(b * nc * GDN_QK_HEADS * CHUNK, LANES)).reshape(a_mat.shape)
    o = _gdn_scan(rhs, qg, kt, attn, tinv, gt, proj, norm_w.reshape(1, GDN_HEAD_DIM), nchunk)
    return o.reshape(b * seq_len, GDN_VAL_DIM)


def _mlstm_kernel(q_ref, k_ref, v_ref, og_ref, gc_ref, gr_ref, brow_ref, bcol_ref, nw_ref, o_ref,
                  c_scr, n_scr, m_scr, *, nchunk):
    i = pl.program_id(0)

    @pl.when(i == 0)
    def _():
        c_scr[...] = jnp.zeros_like(c_scr)
        n_scr[...] = jnp.zeros_like(n_scr)
        m_scr[...] = jnp.zeros_like(m_scr)

    r, c = _tri_incl()
    incl = r >= c
    tri = incl.astype(F32)
    tri_t = (r <= c).astype(F32)
    dk, dv = ML_QK_DIM, ML_V_DIM
    nb = q_ref.shape[0]
    chains = [(bi, h) for bi in range(nb) for h in range(ML_HEADS)]

    def chunk_body(ci, carry):
        rs = pl.ds(pl.multiple_of(ci * CHUNK, CHUNK), CHUNK)
        t = (i * nchunk + ci) * CHUNK + lax.broadcasted_iota(jnp.int32, (CHUNK, 1), 0)
        valid = (t >= N_PAD).astype(F32)
        gate = {}
        for bi in range(nb):
            gcap = ML_GATE_CAP * jnp.tanh((gc_ref[bi, rs, :] + brow_ref[0:1, :]) / ML_GATE_CAP)
            gcap_r = ML_GATE_CAP * jnp.tanh((gr_ref[bi, ci] + bcol_ref[:, 0:1]) / ML_GATE_CAP)
            gate[bi] = (gcap, _dot_f32(tri, jax.nn.log_sigmoid(gcap)),
                        gcap_r, _dot_f32(jax.nn.log_sigmoid(gcap_r), tri_t))
        st = {ch: {} for ch in chains}
        for bi, h in chains:
            s = st[bi, h]
            s["q"] = q_ref[bi, rs, h * dk:(h + 1) * dk] * (dk ** -0.5)
            s["k"] = k_ref[bi, rs, h * dk:(h + 1) * dk] * valid
            s["qb"] = s["q"].astype(BF16)
            s["vb"] = (v_ref[bi, rs, h * dv:(h + 1) * dv] * valid).astype(BF16)
            s["qk"] = _dot_nt(s["qb"], s["k"].astype(BF16))
        for bi, h in chains:
            s = st[bi, h]
            gcap, bcum, gcap_r, bcum_r = gate[bi]
            b_i = bcum[:, ML_HEADS + h:ML_HEADS + h + 1]
            s["bl"] = b_i[CHUNK - 1:CHUNK, :]
            s["ms"] = m_scr[bi * ML_HEADS + h]
            s["a"] = s["bl"] - b_i + gcap[:, h:h + 1]
            s["inter"] = b_i + s["ms"]
            s["d"] = jnp.where(incl, b_i - bcum_r[ML_HEADS + h:ML_HEADS + h + 1, :] + gcap_r[h:h + 1, :], -jnp.inf)
        for ch in chains:
            s = st[ch]
            s["dmax"] = jnp.max(s["d"], -1, keepdims=True)
            s["amax"] = jnp.max(s["a"], 0, keepdims=True)
        for ch in chains:
            s = st[ch]
            s["m_t"] = jnp.maximum(s["dmax"], s["inter"])
            s["m_new"] = jnp.maximum(s["bl"] + s["ms"], s["amax"])
        for ch in chains:
            s = st[ch]
            s["kw"] = s["k"] * jnp.exp(s["a"] - s["m_new"])
            s["w_d"] = jnp.exp(s["d"] - s["m_t"]) * s["qk"]
        for ch in chains:
            s = st[ch]
            s["kv"] = _dot_tn(s["kw"].astype(BF16), s["vb"])
            s["wv"] = _dot(s["w_d"].astype(BF16), s["vb"])
        for bi, h in chains:
            s = st[bi, h]
            s["qc"] = _dot(s["qb"], c_scr[bi * ML_HEADS + h].astype(BF16))
            s["qn"] = jnp.sum(s["q"] * n_scr[bi * ML_HEADS + h], -1, keepdims=True)
            s["wsum"] = jnp.sum(s["w_d"], -1, keepdims=True)
        for ch in chains:
            s = st[ch]
            sc = jnp.exp(s["inter"] - s["m_t"])
            den = sc * s["qn"] + s["wsum"]
            s["hh"] = (sc * s["qc"] + s["wv"]) / jnp.maximum(jnp.abs(den), jnp.exp(-s["m_t"]))
        for bi, h in chains:
            s = st[bi, h]
            si = bi * ML_HEADS + h
            dec = jnp.exp(s["bl"] + s["ms"] - s["m_new"])
            c_scr[si] = dec * c_scr[si] + s["kv"]
            n_scr[si] = dec * n_scr[si] + jnp.sum(s["kw"], 0, keepdims=True)
            m_scr[si] = s["m_new"]
            vs = slice(h * dv, (h + 1) * dv)
            o_ref[bi, rs, vs] = (_rms(s["hh"], nw_ref[:, vs]) * jax.nn.sigmoid(og_ref[bi, rs, vs])).astype(BF16)
        return carry

    lax.fori_loop(0, nchunk, chunk_body, 0)


def _mlstm(a2d, b, seq_len, nw_in, w_in, gate_b, norm_w, nchunk):
    nc = seq_len // CHUNK
    ng = 2 * ML_HEADS
    w_main = w_in[:, :ML_MAIN].astype(BF16)
    w_gate = jnp.pad(w_in[:, ML_MAIN:], ((0, 0), (0, LANES - ng))).astype(BF16)
    proj, gates = _inproj(a2d, nw_in, w_main, w_gate, ML_MAIN // 2)
    proj = proj.reshape(b, seq_len, ML_MAIN)
    gates = gates.reshape(b, seq_len, LANES)
    gates_r = jnp.swapaxes(gates[..., :ng].reshape(b, nc, CHUNK, ng), 2, 3)
    b_row = jnp.zeros((SUBLANES, LANES), F32).at[0, :ng].set(gate_b)
    b_col = jnp.zeros((ng, LANES), F32).at[:, 0].set(gate_b)
    rows = nchunk * CHUNK
    qkb = ML_QK_TOT
    o = pl.pallas_call(
        functools.partial(_mlstm_kernel, nchunk=nchunk),
        grid=(nc // nchunk,),
        in_specs=[
            pl.BlockSpec((b, rows, qkb), lambda i: (0, i, 0)),
            pl.BlockSpec((b, rows, qkb), lambda i: (0, i, 1)),
            pl.BlockSpec((b, rows, ML_V_TOT), lambda i: (0, i, 1)),
            pl.BlockSpec((b, rows, ML_V_TOT), lambda i: (0, i, 2)),
            pl.BlockSpec((b, rows, LANES), lambda i: (0, i, 0)),
            pl.BlockSpec((b, nchunk, ng, CHUNK), lambda i: (0, i, 0, 0)),
            pl.BlockSpec((SUBLANES, LANES), lambda i: (0, 0)),
            pl.BlockSpec((ng, LANES), lambda i: (0, 0)),
            pl.BlockSpec((1, ML_V_TOT), lambda i: (0, 0)),
        ],
        out_specs=pl.BlockSpec((b, rows, ML_V_TOT), lambda i: (0, i, 0)),
        out_shape=jax.ShapeDtypeStruct((b, seq_len, ML_V_TOT), BF16),
        scratch_shapes=[pltpu.VMEM((b * ML_HEADS, ML_QK_DIM, ML_V_DIM), F32),
                        pltpu.VMEM((b * ML_HEADS, 1, ML_QK_DIM), F32),
                        pltpu.VMEM((b * ML_HEADS, 1, 1), F32)],
        compiler_params=_params("arbitrary"),
        name="mlstm",
    )(proj, proj, proj, proj, gates, gates_r, b_row, b_col, norm_w.reshape(1, ML_V_TOT))
    return o.reshape(b * seq_len, ML_V_TOT)


def kernel(x, meta_tokens, norm_w, gdn_w_in, gdn_conv_w, gdn_a_log, gdn_dt_bias, gdn_norm_w, gdn_w_out,
           ml_w_in, ml_gate_b, ml_norm_w, ml_w_out, ffn_w_up, ffn_conv_w, ffn_conv_b, ffn_w_down):
    b, s, d = x.shape
    depth = norm_w.shape[0]
    pad = jnp.zeros((b, N_PAD, d), x.dtype)
    meta = jnp.broadcast_to(meta_tokens[None].astype(x.dtype), (b, N_META, d))
    hs = jnp.concatenate([pad, meta, x], 1)
    seq_len = hs.shape[1]
    nc = seq_len // CHUNK
    nchunk = 3 if nc % 3 == 0 else 1
    for i in range(depth):
        j = i // 2
        nw = norm_w[i].reshape(4, 1, d)
        hs2d = hs.reshape(b * seq_len, d)
        if i % 2 == 0:
            mix = _gated_deltanet(hs2d, b, seq_len, nw[0], gdn_w_in[j], gdn_conv_w[j], gdn_a_log[j],
                                  gdn_dt_bias[j], gdn_norm_w[j], nchunk)
            w_out = gdn_w_out[j]
        else:
            mix = _mlstm(hs2d, b, seq_len, nw[0], ml_w_in[j], ml_gate_b[j], ml_norm_w[j], nchunk)
            w_out = ml_w_out[j]
        hs = _outproj(mix, w_out.astype(BF16), nw[1], hs2d, seq_len).reshape(b, seq_len, d)
        hs = _ffn(hs, nw[2], ffn_w_up[i].astype(BF16), ffn_conv_w[i], ffn_conv_b[i][None],
                  ffn_w_down[i].astype(BF16), nw[3])
    return hs[:, N_PAD + N_META:]
```
